```python
import jax
import jax.numpy as jnp
from jax import lax
import numpy as np

D_MODEL = 2048
BATCH = 16
SEQ = 2048
DEPTH = 2

HEAD_DIM = 128
MEM_TOKENS = 256
MEM_HEADS = 4
MEM_WIDTH = MEM_HEADS * HEAD_DIM
CONV_WIDTH = D_MODEL - MEM_WIDTH
CONV_SIZE = 3
NSA_HEADS = (D_MODEL - MEM_WIDTH) // HEAD_DIM
NSA_KV_HEADS = 4
NSA_GROUP = NSA_HEADS // NSA_KV_HEADS
NSA_WIDTH = NSA_HEADS * HEAD_DIM
KV_BRANCH_WIDTH = NSA_KV_HEADS * HEAD_DIM
KV_WIDTH = 6 * KV_BRANCH_WIDTH
CMP_BLOCK = 32
CMP_STRIDE = 16
CMP_HIDDEN = 256
SLC_BLOCK = 64
SLC_TOPK = 16
WINDOW = 512
Q_BLOCK = 128
SLC_Q_BLOCK = 64
N_A = DEPTH // 2
N_B = DEPTH - N_A
A_IN = 4 * CONV_WIDTH + 2 * MEM_WIDTH
B_IN = 2 * NSA_WIDTH + 3 * NSA_HEADS + 2 * MEM_WIDTH
EPS = 1e-6
NEG = -1e30
FORCE = 1e4
SCALE = HEAD_DIM ** -0.5

kernel_name = 'hybrid_shortconv_nsa_yoco_mem'


def rmsnorm(x, g):
    xf = x.astype(jnp.float32)
    y = xf * lax.rsqrt(jnp.mean(xf * xf, axis=-1, keepdims=True) + EPS)
    return (y * g.astype(jnp.float32)).astype(x.dtype)


def split_cols(u, widths):
    cuts = np.cumsum(widths)[:-1].tolist()
    return jnp.split(u, cuts, axis=-1)


def heads(t, n):
    return t.reshape(t.shape[:-1] + (n, HEAD_DIM))


def masked_softmax(logits, mask):
    l = jnp.where(mask, logits.astype(jnp.float32), NEG)
    p = jax.nn.softmax(l, axis=-1)
    return jnp.where(mask, p, 0.0)


def memory_attention(q_in, mem, mem_norm, w_kv, q_gain, k_gain):
    bn, s = q_in.shape[:2]
    q = rmsnorm(heads(q_in, MEM_HEADS), q_gain)
    mk, mv = jnp.split(rmsnorm(mem, mem_norm) @ w_kv, 2, axis=-1)
    mk = rmsnorm(heads(mk, MEM_HEADS), k_gain)
    mv = heads(mv, MEM_HEADS)
    sc = jnp.einsum('bshd,bmhd->bhsm', q, mk).astype(jnp.float32) * SCALE
    p = jax.nn.softmax(sc, axis=-1).astype(mv.dtype)
    return jnp.einsum('bhsm,bmhd->bshd', p, mv).reshape(bn, s, MEM_WIDTH)


def short_conv(cb, cc, ch, w, b):
    u = cc * ch
    y = lax.conv_general_dilated(
        u, w.astype(u.dtype)[:, None, :], window_strides=(1,),
        padding=[(CONV_SIZE - 1, 0)], dimension_numbers=('NWC', 'WIO', 'NWC'),
        feature_group_count=u.shape[-1])
    return cb * (y + b)


def conv_layer(x, mem, norm, w_in, conv_w, conv_b, w_out, mem_norm, mem_w_kv, mem_qn, mem_kn):
    h = rmsnorm(x, norm)
    cb, cc, ch, cz, mq, mz = split_cols(h @ w_in, [CONV_WIDTH] * 4 + [MEM_WIDTH] * 2)
    y_conv = short_conv(cb, cc, ch, conv_w, conv_b) * jax.nn.silu(cz)
    y_mem = memory_attention(mq, mem, mem_norm, mem_w_kv, mem_qn, mem_kn) * jax.nn.silu(mz)
    return x + jnp.concatenate([y_conv, y_mem], axis=-1) @ w_out


def compress(t, pos, w1, w2):
    bn, s = t.shape[:2]
    n_chunks = s // CMP_STRIDE
    r = CMP_BLOCK // CMP_STRIDE
    c = t.reshape(bn, n_chunks, CMP_STRIDE, NSA_KV_HEADS, HEAD_DIM)
    blk = jnp.concatenate([c[:, i:n_chunks - r + 1 + i] for i in range(r)], axis=2)
    hid = jax.nn.silu(jnp.einsum('bclgd,ldf->bcgf', blk + pos[:, None, :], w1))
    return hid @ w2


def shared_nsa_kv(x, kv_norm, kv_w, pos_k, w1_k, w2_k, pos_v, w1_v, w2_v, kn_cmp, kn_slc, kn_win):
    kv = rmsnorm(x, kv_norm) @ kv_w
    kc, vc, ks, vs, kw, vw = [heads(t, NSA_KV_HEADS) for t in jnp.split(kv, 6, axis=-1)]
    k_cmp = rmsnorm(compress(kc, pos_k, w1_k, w2_k), kn_cmp)
    v_cmp = compress(vc, pos_v, w1_v, w2_v)
    return k_cmp, v_cmp, rmsnorm(ks, kn_slc), vs, rmsnorm(kw, kn_win), vw


def compressed_branch(q, k_cmp, v_cmp):
    s = q.shape[1]
    n_c = k_cmp.shape[1]
    sc = jnp.einsum('bsgjd,bcgd->bgjsc', q, k_cmp).astype(jnp.float32) * SCALE
    t = jnp.arange(s)[:, None]
    c = jnp.arange(n_c)[None, :]
    mask = c * CMP_STRIDE + CMP_BLOCK - 1 <= t
    p = masked_softmax(sc, mask)
    o = jnp.einsum('bgjsc,bcgd->bsgjd', p.astype(v_cmp.dtype), v_cmp)
    return o, p.sum(axis=2)


def select_blocks(importance):
    s, n_c = importance.shape[2:]
    n_s = s // SLC_BLOCK
    n_sel = min(SLC_TOPK, n_s)
    i = jnp.arange(n_c)[:, None]
    j = jnp.arange(n_s)[None, :]
    cover = ((i * CMP_STRIDE < (j + 1) * SLC_BLOCK) &
             (i * CMP_STRIDE + CMP_BLOCK > j * SLC_BLOCK)).astype(jnp.float32)
    score = jnp.einsum('bgsc,cj->bgsj', importance.astype(jnp.float32), cover)
    t = jnp.arange(s)[:, None]
    cur = t // SLC_BLOCK
    allowed = j * SLC_BLOCK <= t
    forced = (j == 0) | (j == cur) | (j == cur - 1)
    score = jnp.where(forced, FORCE, jnp.where(allowed, score, NEG))
    _, idx = lax.top_k(score, n_sel)
    valid = idx * SLC_BLOCK <= jnp.arange(s)[:, None]
    return idx, valid


def selected_branch(q, k_s, v_s, idx, valid):
    bn, s, g, j, d = q.shape
    n_s = s // SLC_BLOCK
    n_q = s // SLC_Q_BLOCK
    n = idx.shape[-1]
    kb = k_s.reshape(bn, n_s, SLC_BLOCK, g, d).transpose(0, 3, 1, 2, 4)
    vb = v_s.reshape(bn, n_s, SLC_BLOCK, g, d).transpose(0, 3, 1, 2, 4)
    qc = q.reshape(bn, n_q, SLC_Q_BLOCK, g, j, d)
    ic = idx.reshape(bn, g, n_q, SLC_Q_BLOCK, n).transpose(0, 2, 1, 3, 4)
    vc = valid.reshape(bn, g, n_q, SLC_Q_BLOCK, n).transpose(0, 2, 1, 3, 4)
    t_c = jnp.arange(s).reshape(n_q, SLC_Q_BLOCK)
    g_ix = jnp.arange(g)[:, None, None]
    offs = jnp.arange(SLC_BLOCK)

    def per_block(q_blk, i_blk, v_blk, t_blk, kbg, vbg):
        tq = t_blk.shape[0]
        k_sel = kbg[g_ix, i_blk]
        v_sel = vbg[g_ix, i_blk].reshape(g, tq, n * SLC_BLOCK, d)
        pos = i_blk[..., None] * SLC_BLOCK + offs
        mask = (pos <= t_blk[None, :, None, None]) & v_blk[..., None]
        sc = jnp.einsum('tgjd,gtnkd->gjtnk', q_blk, k_sel).astype(jnp.float32) * SCALE
        sc = sc.reshape(g, j, tq, n * SLC_BLOCK)
        p = masked_softmax(sc, mask.reshape(g, 1, tq, n * SLC_BLOCK)).astype(v_sel.dtype)
        return jnp.einsum('gjtm,gtmd->tgjd', p, v_sel)

    def per_batch(args):
        q_b, i_b, v_b, kbg, vbg = args
        return lax.map(lambda a: per_block(a[0], a[1], a[2], a[3], kbg, vbg), (q_b, i_b, v_b, t_c))

    o = lax.map(per_batch, (qc, ic, vc, kb, vb))
    return o.reshape(bn, s, g, j, d)


def window_branch(q, k_w, v_w):
    bn, s, g, j, d = q.shape
    n_q = s // Q_BLOCK
    span = WINDOW + Q_BLOCK
    kp = jnp.pad(k_w, ((0, 0), (WINDOW, 0), (0, 0), (0, 0)))
    vp = jnp.pad(v_w, ((0, 0), (WINDOW, 0), (0, 0), (0, 0)))

    def per_block(c):
        start = c * Q_BLOCK
        q_blk = lax.dynamic_slice_in_dim(q, start, Q_BLOCK, axis=1)
        k_blk = lax.dynamic_slice_in_dim(kp, start, span, axis=1)
        v_blk = lax.dynamic_slice_in_dim(vp, start, span, axis=1)
        t = start + jnp.arange(Q_BLOCK)
        p_pos = start - WINDOW + jnp.arange(span)
        diff = t[:, None] - p_pos[None, :]
        mask = (diff >= 0) & (diff < WINDOW) & (p_pos[None, :] >= 0)
        sc = jnp.einsum('btgjd,bpgd->bgjtp', q_blk, k_blk).astype(jnp.float32) * SCALE
        pr = masked_softmax(sc, mask).astype(v_blk.dtype)
        return jnp.einsum('bgjtp,bpgd->btgjd', pr, v_blk)

    o = lax.map(per_block, jnp.arange(n_q))
    return o.transpose(1, 0, 2, 3, 4, 5).reshape(bn, s, g, j, d)


def nsa_layer(x, mem, shared, norm, w_in, gate_bias, q_gain, w_out, mem_norm, mem_w_kv, mem_qn, mem_kn):
    bn, s, _ = x.shape
    h = rmsnorm(x, norm)
    q, gl, z, mq, mz = split_cols(h @ w_in, [NSA_WIDTH, 3 * NSA_HEADS, NSA_WIDTH, MEM_WIDTH, MEM_WIDTH])
    q = rmsnorm(heads(q, NSA_HEADS), q_gain).reshape(bn, s, NSA_KV_HEADS, NSA_GROUP, HEAD_DIM)
    gates = jax.nn.sigmoid(gl + gate_bias).reshape(bn, s, NSA_KV_HEADS, NSA_GROUP, 3)
    k_cmp, v_cmp, k_s, v_s, k_w, v_w = shared
    o_cmp, importance = compressed_branch(q, k_cmp, v_cmp)
    idx, valid = select_blocks(importance)
    o_slc = selected_branch(q, k_s, v_s, idx, valid)
    o_win = window_branch(q, k_w, v_w)
    o = gates[..., 0:1] * o_cmp + gates[..., 1:2] * o_slc + gates[..., 2:3] * o_win
    y_nsa = o.reshape(bn, s, NSA_WIDTH) * jax.nn.silu(z)
    y_mem = memory_attention(mq, mem, mem_norm, mem_w_kv, mem_qn, mem_kn) * jax.nn.silu(mz)
    return x + jnp.concatenate([y_nsa, y_mem], axis=-1) @ w_out


def setup_inputs(seed: int = 0) -> dict:
    key = jax.random.key(seed)
    ks = jax.random.split(key, 32)
    f32 = jnp.float32
    d = D_MODEL

    def nrm(k, shape, scale):
        return jax.random.normal(k, shape, f32) * scale

    def gain(k, shape):
        return 1.0 + 0.01 * jax.random.normal(k, shape, f32)

    cmp_in = (CMP_BLOCK * HEAD_DIM) ** -0.5
    return {
        'x': nrm(ks[0], (BATCH, SEQ, d), 1.0),
        'mem': nrm(ks[1], (BATCH, MEM_TOKENS, d), 1.0),
        'a_norm': gain(ks[2], (N_A, d)),
        'a_w_in': nrm(ks[3], (N_A, d, A_IN), d ** -0.5),
        'a_conv_w': nrm(ks[4], (N_A, CONV_SIZE, CONV_WIDTH), CONV_SIZE ** -0.5),
        'a_conv_b': nrm(ks[5], (N_A, CONV_WIDTH), 0.01),
        'a_w_out': nrm(ks[6], (N_A, d, d), d ** -0.5),
        'kv_norm': gain(ks[7], (d,)),
        'kv_w': nrm(ks[8], (d, KV_WIDTH), d ** -0.5),
        'cmp_pos_k': nrm(ks[9], (CMP_BLOCK, HEAD_DIM), 0.1),
        'cmp_w1_k': nrm(ks[10], (CMP_BLOCK, HEAD_DIM, CMP_HIDDEN), cmp_in),
        'cmp_w2_k': nrm(ks[11], (CMP_HIDDEN, HEAD_DIM), CMP_HIDDEN ** -0.5),
        'cmp_pos_v': nrm(ks[12], (CMP_BLOCK, HEAD_DIM), 0.1),
        'cmp_w1_v': nrm(ks[13], (CMP_BLOCK, HEAD_DIM, CMP_HIDDEN), cmp_in),
        'cmp_w2_v': nrm(ks[14], (CMP_HIDDEN, HEAD_DIM), CMP_HIDDEN ** -0.5),
        'kn_cmp': gain(ks[15], (HEAD_DIM,)),
        'kn_slc': gain(ks[16], (HEAD_DIM,)),
        'kn_win': gain(ks[17], (HEAD_DIM,)),
        'b_norm': gain(ks[18], (N_B, d)),
        'b_w_in': nrm(ks[19], (N_B, d, B_IN), d ** -0.5),
        'b_gate_bias': nrm(ks[20], (N_B, 3 * NSA_HEADS), 0.01),
        'b_q_norm': gain(ks[21], (N_B, HEAD_DIM)),
        'b_w_out': nrm(ks[22], (N_B, d, d), d ** -0.5),
        'mem_norm': gain(ks[23], (DEPTH, d)),
        'mem_w_kv': nrm(ks[24], (DEPTH, d, 2 * MEM_WIDTH), d ** -0.5),
        'mem_q_norm': gain(ks[25], (DEPTH, HEAD_DIM)),
        'mem_k_norm': gain(ks[26], (DEPTH, HEAD_DIM)),
    }


def reference(x, mem, a_norm, a_w_in, a_conv_w, a_conv_b, a_w_out, kv_norm, kv_w,
              cmp_pos_k, cmp_w1_k, cmp_w2_k, cmp_pos_v, cmp_w1_v, cmp_w2_v,
              kn_cmp, kn_slc, kn_win, b_norm, b_w_in, b_gate_bias, b_q_norm, b_w_out,
              mem_norm, mem_w_kv, mem_q_norm, mem_k_norm):
    shared = None
    for layer in range(DEPTH):
        mem_p = (mem_norm[layer], mem_w_kv[layer], mem_q_norm[layer], mem_k_norm[layer])
        if layer < N_A:
            x = conv_layer(x, mem, a_norm[layer], a_w_in[layer], a_conv_w[layer],
                           a_conv_b[layer], a_w_out[layer], *mem_p)
        else:
            if layer == N_A:
                shared = shared_nsa_kv(x, kv_norm, kv_w, cmp_pos_k, cmp_w1_k, cmp_w2_k,
                                       cmp_pos_v, cmp_w1_v, cmp_w2_v, kn_cmp, kn_slc, kn_win)
            i = layer - N_A
            x = nsa_layer(x, mem, shared, b_norm[i], b_w_in[i], b_gate_bias[i], b_q_norm[i],
                          b_w_out[i], *mem_p)
    return x
```

```python
import functools

import numpy as np
import jax
import jax.numpy as jnp
from jax import lax
from jax.experimental import pallas as pl
from jax.experimental.pallas import tpu as pltpu

HEAD_DIM = 128
MEM_TOKENS = 256
MEM_HEADS = 4
MEM_WIDTH = MEM_HEADS * HEAD_DIM
NSA_KV_HEADS = 4
NSA_GROUP = 3
NSA_HEADS = NSA_KV_HEADS * NSA_GROUP
NSA_WIDTH = NSA_HEADS * HEAD_DIM
KV_BRANCH_WIDTH = NSA_KV_HEADS * HEAD_DIM
CONV_SIZE = 3
CMP_BLOCK = 32
CMP_STRIDE = 16
CMP_HIDDEN = 256
SLC_BLOCK = 64
SLC_SHIFT = 6
SLC_TOPK = 16
WINDOW = 512
EPS = 1e-6
NEG = -1e30
FORCE = 1e4
SCALE = HEAD_DIM ** -0.5

V7X_VMEM_BYTES = 64 * 1024 * 1024
V7X_LANES = 128
BF16_SUBLANES = 16

TM = 1024
HALO = BF16_SUBLANES
CONV_CW = 256
TQ = 256
KC = 256
GATE_LANES = 32

BF16 = jnp.bfloat16
F32 = jnp.float32


def _params(vmem_mb, n_grid):
    return pltpu.CompilerParams(
        dimension_semantics=("arbitrary",) * n_grid,
        vmem_limit_bytes=vmem_mb * 1024 * 1024)


def _rms_rows(v, gain):
    return v * lax.rsqrt(jnp.mean(v * v, axis=-1, keepdims=True) + EPS) * gain


def _silu(v):
    return v * jax.nn.sigmoid(v)


def _dot(a, b):
    return jnp.dot(a, b, preferred_element_type=F32)


def _dot_nt(a, b):
    return lax.dot_general(a, b, (((1,), (1,)), ((), ())), preferred_element_type=F32)


def _mem_attention(mq, mk_ref, mv_ref, q_gain):
    outs = []
    for h in range(MEM_HEADS):
        sl = slice(h * HEAD_DIM, (h + 1) * HEAD_DIM)
        q = _rms_rows(mq[:, sl], q_gain * SCALE).astype(BF16)
        s = _dot_nt(q, mk_ref[:, sl])
        e = jnp.exp(s - jnp.max(s, axis=-1, keepdims=True))
        l = jnp.sum(e, axis=-1, keepdims=True)
        outs.append(_dot(e.astype(BF16), mv_ref[:, sl]) / l)
    return outs


def _mem_kv_kernel(mem_ref, g_ref, w_ref, kg_ref, mk_ref, mv_ref):
    h = _rms_rows(mem_ref[...], g_ref[0]).astype(BF16)
    kv = _dot(h, w_ref[0])
    for hd in range(MEM_HEADS):
        sl = slice(hd * HEAD_DIM, (hd + 1) * HEAD_DIM)
        mk_ref[0, :, sl] = _rms_rows(kv[:, sl], kg_ref[0]).astype(BF16)
    mv_ref[0] = kv[:, MEM_WIDTH:].astype(BF16)


def _mem_kv(mem2d, mem_norm, w_kv, k_gain):
    n_layers, d = mem_norm.shape
    rows = mem2d.shape[0]
    tm = min(TM, rows)
    out = jax.ShapeDtypeStruct((n_layers, rows, MEM_WIDTH), BF16)
    return pl.pallas_call(
        _mem_kv_kernel,
        grid=(n_layers, rows // tm),
        in_specs=[
            pl.BlockSpec((tm, d), lambda l, i: (i, 0)),
            pl.BlockSpec((1, 1, d), lambda l, i: (l, 0, 0)),
            pl.BlockSpec((1, d, 2 * MEM_WIDTH), lambda l, i: (l, 0, 0)),
            pl.BlockSpec((1, 1, HEAD_DIM), lambda l, i: (l, 0, 0)),
        ],
        out_specs=[pl.BlockSpec((1, tm, MEM_WIDTH), lambda l, i: (l, i, 0))] * 2,
        out_shape=[out, out],
        compiler_params=_params(40, 2),
        name="mem_kv",
    )(mem2d, mem_norm[:, None, :], w_kv, k_gain[:, None, :])


def _a_in_kernel(x_ref, xh_ref, g_ref, w_ref, cw_ref, cb_ref, mk_ref, mv_ref, qg_ref,
                 yc_ref, ym_ref, h_ref, *, tiles_per_seq, n_conv):
    i = pl.program_id(0)
    c = pl.program_id(1)
    tm = x_ref.shape[0]

    @pl.when(c == 0)
    def _():
        keep = (i % tiles_per_seq != 0).astype(F32)
        h_ref[0:HALO, :] = (_rms_rows(xh_ref[...], g_ref[...]) * keep).astype(BF16)
        h_ref[HALO:, :] = _rms_rows(x_ref[...], g_ref[...]).astype(BF16)

    @pl.when(c < n_conv)
    def _():
        acc = _dot(h_ref[...], w_ref[...])
        cw = CONV_CW
        u = acc[:, cw:2 * cw] * acc[:, 2 * cw:3 * cw]
        taps = cw_ref[...]
        y = (taps[0:1] * pltpu.roll(u, 2, axis=0)[HALO:]
             + taps[1:2] * pltpu.roll(u, 1, axis=0)[HALO:]
             + taps[2:3] * u[HALO:])
        yc_ref[...] = (acc[HALO:, 0:cw] * (y + cb_ref[...])
                       * _silu(acc[HALO:, 3 * cw:4 * cw])).astype(BF16)

    @pl.when(c == n_conv)
    def _():
        acc = _dot(h_ref[HALO:, :], w_ref[...])
        outs = _mem_attention(acc[:, :MEM_WIDTH], mk_ref, mv_ref, qg_ref[...])
        for hd in range(MEM_HEADS):
            sl = slice(hd * HEAD_DIM, (hd + 1) * HEAD_DIM)
            ym_ref[:, sl] = (outs[hd] * _silu(acc[:, MEM_WIDTH + hd * HEAD_DIM:
                                                  MEM_WIDTH + (hd + 1) * HEAD_DIM])).astype(BF16)


def _a_in(x2d, norm, w_perm, conv_w, conv_b, mk, mv, q_gain, seq):
    t, d = x2d.shape
    conv_width = conv_w.shape[1]
    n_conv = conv_width // CONV_CW
    tn = 4 * CONV_CW
    assert w_perm.shape[1] == (n_conv + 1) * tn and 2 * MEM_WIDTH == tn
    tiles_per_seq = seq // TM
    halo_blocks = TM // HALO
    kern = functools.partial(_a_in_kernel, tiles_per_seq=tiles_per_seq, n_conv=n_conv)
    last = n_conv - 1
    return pl.pallas_call(
        kern,
        grid=(t // TM, n_conv + 1),
        in_specs=[
            pl.BlockSpec((TM, d), lambda i, c: (i, 0)),
            pl.BlockSpec((HALO, d), lambda i, c: (jnp.maximum(i * halo_blocks - 1, 0), 0)),
            pl.BlockSpec((1, d), lambda i, c: (0, 0)),
            pl.BlockSpec((d, tn), lambda i, c: (0, c)),
            pl.BlockSpec((CONV_SIZE, CONV_CW), lambda i, c: (0, jnp.minimum(c, last))),
            pl.BlockSpec((1, CONV_CW), lambda i, c: (0, jnp.minimum(c, last))),
            pl.BlockSpec((MEM_TOKENS, MEM_WIDTH), lambda i, c: (i // tiles_per_seq, 0)),
            pl.BlockSpec((MEM_TOKENS, MEM_WIDTH), lambda i, c: (i // tiles_per_seq, 0)),
            pl.BlockSpec((1, HEAD_DIM), lambda i, c: (0, 0)),
        ],
        out_specs=[
            pl.BlockSpec((TM, CONV_CW), lambda i, c: (i, jnp.minimum(c, last))),
            pl.BlockSpec((TM, MEM_WIDTH), lambda i, c: (i, 0)),
        ],
        out_shape=[jax.ShapeDtypeStruct((t, conv_width), BF16),
                   jax.ShapeDtypeStruct((t, MEM_WIDTH), BF16)],
        scratch_shapes=[pltpu.VMEM((TM + HALO, d), BF16)],
        compiler_params=_params(56, 2),
        name="a_in",
    )(x2d, x2d, norm[None, :], w_perm, conv_w, conv_b[None, :], mk, mv, q_gain[None, :])


def _out_proj_kernel(res_ref, y1_ref, y2_ref, w1_ref, w2_ref, o_ref):
    o_ref[...] = res_ref[...] + _dot(y1_ref[...], w1_ref[...]) + _dot(y2_ref[...], w2_ref[...])


def _out_proj(res, y1, y2, w1, w2):
    t, d = res.shape
    tn = 1024
    n1, n2 = y1.shape[1], y2.shape[1]
    return pl.pallas_call(
        _out_proj_kernel,
        grid=(t // TM, d // tn),
        in_specs=[
            pl.BlockSpec((TM, tn), lambda i, j: (i, j)),
            pl.BlockSpec((TM, n1), lambda i, j: (i, 0)),
            pl.BlockSpec((TM, n2), lambda i, j: (i, 0)),
            pl.BlockSpec((n1, tn), lambda i, j: (0, j)),
            pl.BlockSpec((n2, tn), lambda i, j: (0, j)),
        ],
        out_specs=pl.BlockSpec((TM, tn), lambda i, j: (i, j)),
        out_shape=jax.ShapeDtypeStruct((t, d), F32),
        compiler_params=_params(48, 2),
        name="out_proj",
    )(res, y1, y2, w1, w2)


def _store_transposed(dst_ref, val):
    vt = val.T.astype(BF16)
    for g in range(NSA_KV_HEADS):
        for cc in range(val.shape[0] // KC):
            dst_ref[0, g, cc] = vt[g * HEAD_DIM:(g + 1) * HEAD_DIM, cc * KC:(cc + 1) * KC]


def _nsa_kv_kernel(x_ref, g_ref, w_ref, kns_ref, knw_ref,
                   kc_ref, vc_ref, ks_ref, vst_ref, kw_ref, vwt_ref, h_ref):
    j = pl.program_id(1)

    @pl.when(j == 0)
    def _():
        h_ref[...] = _rms_rows(x_ref[...], g_ref[...]).astype(BF16)

    def normed_keys(acc, gain):
        return jnp.concatenate(
            [_rms_rows(acc[:, g * HEAD_DIM:(g + 1) * HEAD_DIM], gain).astype(BF16)
             for g in range(NSA_KV_HEADS)], axis=1)

    @pl.when(j == 0)
    def _():
        kc_ref[...] = _dot(h_ref[...], w_ref[...])

    @pl.when(j == 1)
    def _():
        vc_ref[...] = _dot(h_ref[...], w_ref[...])

    @pl.when(j == 2)
    def _():
        ks_ref[...] = normed_keys(_dot(h_ref[...], w_ref[...]), kns_ref[...])

    @pl.when(j == 3)
    def _():
        _store_transposed(vst_ref, _dot(h_ref[...], w_ref[...]))

    @pl.when(j == 4)
    def _():
        kw_ref[...] = normed_keys(_dot(h_ref[...], w_ref[...]), knw_ref[...])

    @pl.when(j == 5)
    def _():
        _store_transposed(vwt_ref, _dot(h_ref[...], w_ref[...]))


def _nsa_kv(x2d, norm, w, kn_slc, kn_win, batch, seq):
    t, d = x2d.shape
    bw = KV_BRANCH_WIDTH
    tiles_per_seq = seq // TM
    row = lambda i, j: (i, 0)
    vt_spec = pl.BlockSpec((1, NSA_KV_HEADS, TM // KC, HEAD_DIM, KC),
                           lambda i, j: (i // tiles_per_seq, 0, i % tiles_per_seq, 0, 0))
    vt_shape = jax.ShapeDtypeStruct((batch, NSA_KV_HEADS, seq // KC, HEAD_DIM, KC), BF16)
    return pl.pallas_call(
        _nsa_kv_kernel,
        grid=(t // TM, 6),
        in_specs=[
            pl.BlockSpec((TM, d), row),
            pl.BlockSpec((1, d), lambda i, j: (0, 0)),
            pl.BlockSpec((d, bw), lambda i, j: (0, j)),
            pl.BlockSpec((1, HEAD_DIM), lambda i, j: (0, 0)),
            pl.BlockSpec((1, HEAD_DIM), lambda i, j: (0, 0)),
        ],
        out_specs=[pl.BlockSpec((TM, bw), row), pl.BlockSpec((TM, bw), row),
                   pl.BlockSpec((TM, bw), row), vt_spec,
                   pl.BlockSpec((TM, bw), row), vt_spec],
        out_shape=[jax.ShapeDtypeStruct((t, bw), F32), jax.ShapeDtypeStruct((t, bw), F32),
                   jax.ShapeDtypeStruct((t, bw), BF16), vt_shape,
                   jax.ShapeDtypeStruct((t, bw), BF16), vt_shape],
        scratch_shapes=[pltpu.VMEM((TM, d), BF16)],
        compiler_params=_params(48, 2),
        name="nsa_kv",
    )(x2d, norm[None, :], w, kn_slc[None, :], kn_win[None, :])


def _compress_kernel(x_ref, pa_ref, pb_ref, w1a_ref, w1b_ref, w2_ref, kn_ref, o_ref,
                     *, is_key, segs):
    x = x_ref[...]
    rows = x.shape[0]
    n_chunks = rows // segs
    a = _dot((x + pa_ref[...]).astype(BF16), w1a_ref[...])
    b = _dot((x + pb_ref[...]).astype(BF16), w1b_ref[...])
    hid = _silu(a + pltpu.roll(b, rows - 1, axis=0))
    o = _dot(hid.astype(BF16), w2_ref[...])
    if is_key:
        o = _rms_rows(o, kn_ref[...])
    chunk = lax.broadcasted_iota(jnp.int32, o.shape, 0) & (n_chunks - 1)
    o = jnp.where(chunk == n_chunks - 1, 0.0, o)
    if is_key:
        o_ref[...] = o.astype(BF16)
    else:
        for s in range(segs):
            o_ref[s] = o[s * n_chunks:(s + 1) * n_chunks, :].T.astype(BF16)


def _compress(xc, pos, w1, w2, kn, is_key, n_chunks):
    rows, width = xc.shape
    segs = min(8, rows // n_chunks)
    tr = segs * n_chunks
    half = CMP_STRIDE
    pa = pos[:half].reshape(1, width)
    pb = pos[half:].reshape(1, width)
    w1a = w1[:half].reshape(width, CMP_HIDDEN).astype(BF16)
    w1b = w1[half:].reshape(width, CMP_HIDDEN).astype(BF16)
    kern = functools.partial(_compress_kernel, is_key=is_key, segs=segs)
    const = lambda i: (0, 0)
    if is_key:
        out_spec = pl.BlockSpec((tr, HEAD_DIM), lambda i: (i, 0))
        out_shape = jax.ShapeDtypeStruct((rows, HEAD_DIM), BF16)
    else:
        out_spec = pl.BlockSpec((segs, HEAD_DIM, n_chunks), lambda i: (i, 0, 0))
        out_shape = jax.ShapeDtypeStruct((rows // n_chunks, HEAD_DIM, n_chunks), BF16)
    return pl.pallas_call(
        kern,
        grid=(rows // tr,),
        in_specs=[
            pl.BlockSpec((tr, width), lambda i: (i, 0)),
            pl.BlockSpec((1, width), const), pl.BlockSpec((1, width), const),
            pl.BlockSpec((width, CMP_HIDDEN), const), pl.BlockSpec((width, CMP_HIDDEN), const),
            pl.BlockSpec((CMP_HIDDEN, HEAD_DIM), const),
            pl.BlockSpec((1, HEAD_DIM), const),
        ],
        out_specs=out_spec,
        out_shape=out_shape,
        compiler_params=_params(40, 1),
        name="compress_k" if is_key else "compress_v",
    )(xc, pa, pb, w1a, w1b, w2.astype(BF16), kn[None, :])


def _b_in_kernel(x_ref, g_ref, w_ref, wg_ref, gb_ref, qg_ref, mk_ref, mv_ref, mqg_ref,
                 qt_ref, gate_ref, sz_ref, ym_ref, h_ref, om_ref):
    j = pl.program_id(1)
    tm = x_ref.shape[0]
    heads_per_step = w_ref.shape[1] // HEAD_DIM

    @pl.when(j == 0)
    def _():
        h_ref[...] = _rms_rows(x_ref[...], g_ref[...]).astype(BF16)
        gate_ref[...] = jax.nn.sigmoid(_dot(h_ref[...], wg_ref[...]) + gb_ref[...])

    @pl.when(j < 3)
    def _():
        acc = _dot(h_ref[...], w_ref[...])
        gain = jnp.tile(qg_ref[...], (1, tm // HEAD_DIM))
        for hd in range(heads_per_step):
            qt = acc[:, hd * HEAD_DIM:(hd + 1) * HEAD_DIM].T
            inv = lax.rsqrt(jnp.mean(qt * qt, axis=0, keepdims=True) + EPS)
            qt_ref[0, hd * HEAD_DIM:(hd + 1) * HEAD_DIM, :] = (qt * inv * gain).astype(BF16)

    @pl.when((j >= 3) & (j < 6))
    def _():
        sz_ref[...] = _silu(_dot(h_ref[...], w_ref[...])).astype(BF16)

    @pl.when(j == 6)
    def _():
        outs = _mem_attention(_dot(h_ref[...], w_ref[...]), mk_ref, mv_ref, mqg_ref[...])
        for hd in range(MEM_HEADS):
            om_ref[:, hd * HEAD_DIM:(hd + 1) * HEAD_DIM] = outs[hd]

    @pl.when(j == 7)
    def _():
        ym_ref[...] = (om_ref[...] * _silu(_dot(h_ref[...], w_ref[...]))).astype(BF16)


def _b_in(x2d, norm, w_main, w_gate, gate_bias, q_gain_b, mk, mv, mem_q_gain, batch, seq):
    t, d = x2d.shape
    tn = 512
    tiles_per_seq = seq // TM
    const = lambda i, j: (0, 0)
    mem_idx = lambda i, j: (i // tiles_per_seq, 0)
    return pl.pallas_call(
        _b_in_kernel,
        grid=(t // TM, 8),
        in_specs=[
            pl.BlockSpec((TM, d), lambda i, j: (i, 0)),
            pl.BlockSpec((1, d), const),
            pl.BlockSpec((d, tn), lambda i, j: (0, j)),
            pl.BlockSpec((d, V7X_LANES), const),
            pl.BlockSpec((1, V7X_LANES), const),
            pl.BlockSpec((HEAD_DIM, HEAD_DIM), const),
            pl.BlockSpec((MEM_TOKENS, MEM_WIDTH), mem_idx),
            pl.BlockSpec((MEM_TOKENS, MEM_WIDTH), mem_idx),
            pl.BlockSpec((1, HEAD_DIM), const),
        ],
        out_specs=[
            pl.BlockSpec((1, tn, TM), lambda i, j: (i // tiles_per_seq, jnp.minimum(j, 2),
                                                     i % tiles_per_seq)),
            pl.BlockSpec((TM, V7X_LANES), lambda i, j: (i, 0)),
            pl.BlockSpec((TM, tn), lambda i, j: (i, jnp.clip(j - 3, 0, 2))),
            pl.BlockSpec((TM, MEM_WIDTH), lambda i, j: (i, 0)),
        ],
        out_shape=[jax.ShapeDtypeStruct((batch, NSA_WIDTH, seq), BF16),
                   jax.ShapeDtypeStruct((t, V7X_LANES), F32),
                   jax.ShapeDtypeStruct((t, NSA_WIDTH), BF16),
                   jax.ShapeDtypeStruct((t, MEM_WIDTH), BF16)],
        scratch_shapes=[pltpu.VMEM((TM, d), BF16), pltpu.VMEM((TM, MEM_WIDTH), F32)],
        compiler_params=_params(48, 2),
        name="b_in",
    )(x2d, norm[None, :], w_main, w_gate, gate_bias, q_gain_b, mk, mv, mem_q_gain[None, :])


def _softmax_chunk(s, m, l, acc, vt):
    m_new = jnp.maximum(m, jnp.max(s, axis=0, keepdims=True))
    alpha = jnp.exp(m - m_new)
    p = jnp.exp(s - m_new)
    l = alpha * l + jnp.sum(p, axis=0, keepdims=True)
    acc = alpha * acc + _dot(vt, p.astype(BF16))
    return m_new, l, acc


def _nsa_attn_kernel(qt_ref, kcmp_ref, vcmpt_ref, ks_ref, vst_ref, kw_ref, vwt_ref,
                     gate_ref, sz_ref, cover_ref, expand_ref, y_ref, bias_ref, gt_ref):
    g = pl.program_id(1)
    qi = pl.program_id(2)
    n = NSA_GROUP * TQ
    qs = qi * TQ
    q3 = qt_ref[0]
    qt = jnp.concatenate([q3[j * HEAD_DIM:(j + 1) * HEAD_DIM] for j in range(NSA_GROUP)],
                         axis=1)

    n_cmp = kcmp_ref.shape[0]
    s = _dot(kcmp_ref[...], qt)
    c_idx = lax.broadcasted_iota(jnp.int32, (n_cmp, n), 0)
    t_idx = qs + (lax.broadcasted_iota(jnp.int32, (n_cmp, n), 1) & (TQ - 1))
    cmask = c_idx * CMP_STRIDE + (CMP_BLOCK - 1) <= t_idx
    sm = jnp.where(cmask, s, NEG)
    e = jnp.exp(sm - jnp.max(sm, axis=0, keepdims=True))
    p = jnp.where(cmask, e / jnp.sum(e, axis=0, keepdims=True), 0.0)
    o_cmp = _dot(vcmpt_ref[0], p.astype(BF16))
    imp = p[:, 0:TQ] + p[:, TQ:2 * TQ] + p[:, 2 * TQ:3 * TQ]

    hi = imp.astype(BF16)
    r1 = imp - hi.astype(F32)
    mid = r1.astype(BF16)
    lo = (r1 - mid.astype(F32)).astype(BF16)
    cover = cover_ref[...]
    score = _dot(cover, hi) + _dot(cover, mid) + _dot(cover, lo)
    n_sel = score.shape[0]
    j_idx = lax.broadcasted_iota(jnp.int32, (n_sel, TQ), 0)
    tq_idx = qs + lax.broadcasted_iota(jnp.int32, (n_sel, TQ), 1)
    cur = tq_idx >> SLC_SHIFT
    allowed = j_idx * SLC_BLOCK <= tq_idx
    forced = (j_idx == 0) | (j_idx == cur) | (j_idx == cur - 1)
    score = jnp.where(forced, FORCE, jnp.where(allowed, score, NEG))
    rank = jnp.zeros((n_sel, TQ), jnp.int32)
    for jp in range(n_sel):
        row = score[jp:jp + 1, :]
        before = (row > score) | ((row == score) & (j_idx > jp))
        rank = rank + before.astype(jnp.int32)
    sel = jnp.where(allowed & (rank < SLC_TOPK), 1.0, 0.0).astype(BF16)
    sel3 = jnp.concatenate([sel] * NSA_GROUP, axis=1)
    bias_ref[...] = (_dot(expand_ref[...], sel3) - 1.0) * (-NEG)

    k_loc = lax.broadcasted_iota(jnp.int32, (KC, n), 0)
    t_loc = lax.broadcasted_iota(jnp.int32, (KC, n), 1) & (TQ - 1)
    causal = k_loc <= t_loc
    rows_per_chunk = (KC // SLC_BLOCK) * 8
    sub = SLC_BLOCK // 8

    def sel_scores(ck):
        k = ks_ref[pl.ds(pl.multiple_of(ck * KC, KC), KC), :]
        sc = _dot(k, qt)
        b = bias_ref[pl.ds(pl.multiple_of(ck * rows_per_chunk, rows_per_chunk), rows_per_chunk), :]
        sc = sc.reshape(KC // SLC_BLOCK, sub, 8, n) + b.reshape(KC // SLC_BLOCK, 1, 8, n)
        return sc.reshape(KC, n)

    init = (jnp.full((1, n), NEG, F32), jnp.zeros((1, n), F32), jnp.zeros((HEAD_DIM, n), F32))

    def sel_body(ck, carry):
        return _softmax_chunk(sel_scores(ck), *carry, vst_ref[0, 0, ck])

    carry = lax.fori_loop(0, qi, sel_body, init)
    m_s, l_s, acc_s = _softmax_chunk(jnp.where(causal, sel_scores(qi), NEG), *carry,
                                     vst_ref[0, 0, qi])
    o_sel = acc_s / l_s

    band = k_loc > t_loc

    def win_body(ck, carry):
        k = kw_ref[pl.ds(pl.multiple_of(ck * KC, KC), KC), :]
        sc = _dot(k, qt)
        sc = jnp.where(band | (ck > qi - 2), sc, NEG)
        return _softmax_chunk(sc, *carry, vwt_ref[0, 0, ck])

    carry = lax.fori_loop(jnp.maximum(qi - 2, 0), qi, win_body, init)
    k = kw_ref[pl.ds(pl.multiple_of(qi * KC, KC), KC), :]
    m_w, l_w, acc_w = _softmax_chunk(jnp.where(causal, _dot(k, qt), NEG), *carry,
                                     vwt_ref[0, 0, qi])
    o_win = acc_w / l_w

    gt_ref[...] = gate_ref[...].T
    gates = gt_ref[pl.ds(pl.multiple_of(g * GATE_LANES, GATE_LANES), GATE_LANES), :]
    for j in range(NSA_GROUP):
        cols = slice(j * TQ, (j + 1) * TQ)
        yt = (gates[3 * j:3 * j + 1] * o_cmp[:, cols]
              + gates[3 * j + 1:3 * j + 2] * o_sel[:, cols]
              + gates[3 * j + 2:3 * j + 3] * o_win[:, cols])
        ch = slice(j * HEAD_DIM, (j + 1) * HEAD_DIM)
        y_ref[:, ch] = (yt.T * sz_ref[:, ch].astype(F32)).astype(BF16)


def _nsa_attn(qt, kcmp, vcmpt, ks, vst, kw, vwt, gates, sz, batch, seq):
    n_q = seq // TQ
    n_cmp = seq // CMP_STRIDE
    n_sel = seq // SLC_BLOCK
    n = NSA_GROUP * TQ
    i = np.arange(n_cmp)[:, None]
    j = np.arange(n_sel)[None, :]
    cover = ((i * CMP_STRIDE < (j + 1) * SLC_BLOCK) & (i * CMP_STRIDE + CMP_BLOCK > j * SLC_BLOCK))
    cover_t = jnp.asarray(cover.T, BF16)
    expand = jnp.asarray(np.arange(n_sel * 8)[:, None] // 8 == np.arange(n_sel)[None, :], BF16)
    kv_spec = pl.BlockSpec((seq, HEAD_DIM), lambda b, g, q: (b, g))
    vt_spec = pl.BlockSpec((1, 1, seq // KC, HEAD_DIM, KC), lambda b, g, q: (b, g, 0, 0, 0))
    gw = NSA_GROUP * HEAD_DIM
    return pl.pallas_call(
        _nsa_attn_kernel,
        grid=(batch, NSA_KV_HEADS, n_q),
        in_specs=[
            pl.BlockSpec((1, gw, TQ), lambda b, g, q: (b, g, q)),
            pl.BlockSpec((n_cmp, HEAD_DIM), lambda b, g, q: (b * NSA_KV_HEADS + g, 0)),
            pl.BlockSpec((1, HEAD_DIM, n_cmp), lambda b, g, q: (b * NSA_KV_HEADS + g, 0, 0)),
            kv_spec, vt_spec, kv_spec, vt_spec,
            pl.BlockSpec((TQ, V7X_LANES), lambda b, g, q: (b * n_q + q, 0)),
            pl.BlockSpec((TQ, gw), lambda b, g, q: (b * n_q + q, g)),
            pl.BlockSpec((n_sel, n_cmp), lambda b, g, q: (0, 0)),
            pl.BlockSpec((n_sel * 8, n_sel), lambda b, g, q: (0, 0)),
        ],
        out_specs=pl.BlockSpec((TQ, gw), lambda b, g, q: (b * n_q + q, g)),
        out_shape=jax.ShapeDtypeStruct((batch * seq, NSA_WIDTH), BF16),
        scratch_shapes=[pltpu.VMEM((n_sel * 8, n), F32), pltpu.VMEM((V7X_LANES, TQ), F32)],
        compiler_params=_params(40, 3),
        name="nsa_attn",
    )(qt, kcmp, vcmpt, ks, vst, kw, vwt, gates, sz, cover_t, expand)


def _permute_a_w_in(w, conv_width):
    n_conv = conv_width // CONV_CW
    parts = [w[:, k * conv_width:(k + 1) * conv_width].reshape(-1, n_conv, CONV_CW)
             for k in range(4)]
    conv = jnp.stack(parts, axis=2).reshape(w.shape[0], 4 * conv_width)
    return jnp.concatenate([conv, w[:, 4 * conv_width:]], axis=1).astype(BF16)


def _gate_layout(v):
    per_group = NSA_GROUP * 3
    v = v.reshape(v.shape[:-1] + (NSA_KV_HEADS, per_group))
    v = jnp.pad(v, [(0, 0)] * (v.ndim - 1) + [(0, GATE_LANES - per_group)])
    return v.reshape(v.shape[:-2] + (NSA_KV_HEADS * GATE_LANES,))


def kernel(x, mem, a_norm, a_w_in, a_conv_w, a_conv_b, a_w_out, kv_norm, kv_w,
           cmp_pos_k, cmp_w1_k, cmp_w2_k, cmp_pos_v, cmp_w1_v, cmp_w2_v,
           kn_cmp, kn_slc, kn_win, b_norm, b_w_in, b_gate_bias, b_q_norm, b_w_out,
           mem_norm, mem_w_kv, mem_q_norm, mem_k_norm):
    batch, seq, d = x.shape
    conv_width = d - MEM_WIDTH
    assert a_norm.shape[0] == 1 and b_norm.shape[0] == 1 and seq % TM == 0
    x2d = x.reshape(batch * seq, d)
    mem2d = mem.reshape(batch * MEM_TOKENS, d)

    mk, mv = _mem_kv(mem2d, mem_norm, mem_w_kv.astype(BF16), mem_k_norm)

    y_conv, y_mem = _a_in(x2d, a_norm[0], _permute_a_w_in(a_w_in[0], conv_width), a_conv_w[0],
                          a_conv_b[0], mk[0], mv[0], mem_q_norm[0], seq)
    w_out = a_w_out[0].astype(BF16)
    x1 = _out_proj(x2d, y_conv, y_mem, w_out[:conv_width], w_out[conv_width:])

    kc, vc, ks, vst, kw, vwt = _nsa_kv(x1, kv_norm, kv_w.astype(BF16), kn_slc, kn_win, batch, seq)
    n_chunks = seq // CMP_STRIDE

    def chunked(t):
        t = t.reshape(batch, n_chunks, CMP_STRIDE, NSA_KV_HEADS, HEAD_DIM)
        return t.transpose(0, 3, 1, 2, 4).reshape(batch * NSA_KV_HEADS * n_chunks,
                                                  CMP_STRIDE * HEAD_DIM)

    k_cmp = _compress(chunked(kc), cmp_pos_k, cmp_w1_k, cmp_w2_k, kn_cmp, True, n_chunks)
    v_cmp_t = _compress(chunked(vc), cmp_pos_v, cmp_w1_v, cmp_w2_v, kn_cmp, False, n_chunks)

    wb = b_w_in[0]
    o = 0
    w_q = wb[:, o:o + NSA_WIDTH]; o += NSA_WIDTH
    w_g = wb[:, o:o + 3 * NSA_HEADS]; o += 3 * NSA_HEADS
    w_z = wb[:, o:o + NSA_WIDTH]; o += NSA_WIDTH
    w_m = wb[:, o:]
    w_main = jnp.concatenate([w_q, w_z, w_m], axis=1).astype(BF16)
    q_gain_b = jnp.broadcast_to((b_q_norm[0] * SCALE)[:, None], (HEAD_DIM, HEAD_DIM))
    qt, gates, sz, y_mem2 = _b_in(x1, b_norm[0], w_main, _gate_layout(w_g).astype(BF16),
                                  _gate_layout(b_gate_bias[0])[None, :], q_gain_b,
                                  mk[1], mv[1], mem_q_norm[1], batch, seq)
    y_nsa = _nsa_attn(qt, k_cmp, v_cmp_t, ks, vst, kw, vwt, gates, sz, batch, seq)
    w_out2 = b_w_out[0].astype(BF16)
    out = _out_proj(x1, y_nsa, y_mem2, w_out2[:NSA_WIDTH], w_out2[NSA_WIDTH:])
    return out.reshape(batch, seq, d)
```

```python
import functools

import numpy as np
import jax
import jax.numpy as jnp
from jax import lax
from jax.experimental import pallas as pl
from jax.experimental.pallas import tpu as pltpu

HEAD_DIM = 128
MEM_TOKENS = 256
MEM_HEADS = 4
MEM_WIDTH = MEM_HEADS * HEAD_DIM
NSA_KV_HEADS = 4
NSA_GROUP = 3
NSA_HEADS = NSA_KV_HEADS * NSA_GROUP
NSA_WIDTH = NSA_HEADS * HEAD_DIM
KV_BRANCH_WIDTH = NSA_KV_HEADS * HEAD_DIM
CONV_SIZE = 3
CMP_BLOCK = 32
CMP_STRIDE = 16
CMP_HIDDEN = 256
SLC_BLOCK = 64
SLC_SHIFT = 6
SLC_TOPK = 16
WINDOW = 512
EPS = 1e-6
NEG = -1e30
FORCE = 1e4
SCALE = HEAD_DIM ** -0.5
LOG2E = 1.4426950408889634

V7X_VMEM_BYTES = 64 * 1024 * 1024
V7X_LANES = 128
BF16_SUBLANES = 16

TM = 1024
HALO = BF16_SUBLANES
CONV_CW = 256
TQ = 256
KC = 256
GATE_LANES = 32

BF16 = jnp.bfloat16
F32 = jnp.float32


def _params(vmem_mb, n_grid):
    return pltpu.CompilerParams(
        dimension_semantics=("arbitrary",) * n_grid,
        vmem_limit_bytes=vmem_mb * 1024 * 1024)


def _rms_rows(v, gain):
    return v * lax.rsqrt(jnp.mean(v * v, axis=-1, keepdims=True) + EPS) * gain


def _silu(v):
    return v * jax.nn.sigmoid(v)


def _dot(a, b):
    return jnp.dot(a, b, preferred_element_type=F32)


def _dot_nt(a, b):
    return lax.dot_general(a, b, (((1,), (1,)), ((), ())), preferred_element_type=F32)


def _mem_attention(mq, mk_ref, mv_ref, q_gain):
    outs = []
    for h in range(MEM_HEADS):
        sl = slice(h * HEAD_DIM, (h + 1) * HEAD_DIM)
        q = _rms_rows(mq[:, sl], q_gain * SCALE).astype(BF16)
        s = _dot_nt(q, mk_ref[:, sl])
        e = jnp.exp(s - jnp.max(s, axis=-1, keepdims=True))
        l = jnp.sum(e, axis=-1, keepdims=True)
        outs.append(_dot(e.astype(BF16), mv_ref[:, sl]) / l)
    return outs


def _mem_kv_kernel(mem_ref, g_ref, w_ref, kg_ref, mk_ref, mv_ref):
    h = _rms_rows(mem_ref[...], g_ref[0]).astype(BF16)
    kv = _dot(h, w_ref[0])
    for hd in range(MEM_HEADS):
        sl = slice(hd * HEAD_DIM, (hd + 1) * HEAD_DIM)
        mk_ref[0, :, sl] = _rms_rows(kv[:, sl], kg_ref[0]).astype(BF16)
    mv_ref[0] = kv[:, MEM_WIDTH:].astype(BF16)


def _mem_kv(mem2d, mem_norm, w_kv, k_gain):
    n_layers, d = mem_norm.shape
    rows = mem2d.shape[0]
    tm = min(TM, rows)
    out = jax.ShapeDtypeStruct((n_layers, rows, MEM_WIDTH), BF16)
    return pl.pallas_call(
        _mem_kv_kernel,
        grid=(n_layers, rows // tm),
        in_specs=[
            pl.BlockSpec((tm, d), lambda l, i: (i, 0)),
            pl.BlockSpec((1, 1, d), lambda l, i: (l, 0, 0)),
            pl.BlockSpec((1, d, 2 * MEM_WIDTH), lambda l, i: (l, 0, 0)),
            pl.BlockSpec((1, 1, HEAD_DIM), lambda l, i: (l, 0, 0)),
        ],
        out_specs=[pl.BlockSpec((1, tm, MEM_WIDTH), lambda l, i: (l, i, 0))] * 2,
        out_shape=[out, out],
        compiler_params=_params(40, 2),
        name="mem_kv",
    )(mem2d, mem_norm[:, None, :], w_kv, k_gain[:, None, :])


def _a_in_kernel(x_ref, xh_ref, g_ref, w_ref, cw_ref, cb_ref, mk_ref, mv_ref, qg_ref,
                 yc_ref, ym_ref, h_ref, *, tiles_per_seq, n_conv):
    i = pl.program_id(0)
    c = pl.program_id(1)
    tm = x_ref.shape[0]

    @pl.when(c == 0)
    def _():
        keep = (i % tiles_per_seq != 0).astype(F32)
        h_ref[0:HALO, :] = (_rms_rows(xh_ref[...], g_ref[...]) * keep).astype(BF16)
        h_ref[HALO:, :] = _rms_rows(x_ref[...], g_ref[...]).astype(BF16)

    @pl.when(c < n_conv)
    def _():
        acc = _dot(h_ref[...], w_ref[...])
        cw = CONV_CW
        u = acc[:, cw:2 * cw] * acc[:, 2 * cw:3 * cw]
        taps = cw_ref[...]
        y = (taps[0:1] * pltpu.roll(u, 2, axis=0)[HALO:]
             + taps[1:2] * pltpu.roll(u, 1, axis=0)[HALO:]
             + taps[2:3] * u[HALO:])
        yc_ref[...] = (acc[HALO:, 0:cw] * (y + cb_ref[...])
                       * _silu(acc[HALO:, 3 * cw:4 * cw])).astype(BF16)

    @pl.when(c == n_conv)
    def _():
        acc = _dot(h_ref[HALO:, :], w_ref[...])
        outs = _mem_attention(acc[:, :MEM_WIDTH], mk_ref, mv_ref, qg_ref[...])
        for hd in range(MEM_HEADS):
            sl = slice(hd * HEAD_DIM, (hd + 1) * HEAD_DIM)
            ym_ref[:, sl] = (outs[hd] * _silu(acc[:, MEM_WIDTH + hd * HEAD_DIM:
                                                  MEM_WIDTH + (hd + 1) * HEAD_DIM])).astype(BF16)


def _a_in(x2d, norm, w_perm, conv_w, conv_b, mk, mv, q_gain, seq):
    t, d = x2d.shape
    conv_width = conv_w.shape[1]
    n_conv = conv_width // CONV_CW
    tn = 4 * CONV_CW
    assert w_perm.shape[1] == (n_conv + 1) * tn and 2 * MEM_WIDTH == tn
    tiles_per_seq = seq // TM
    halo_blocks = TM // HALO
    kern = functools.partial(_a_in_kernel, tiles_per_seq=tiles_per_seq, n_conv=n_conv)
    last = n_conv - 1
    return pl.pallas_call(
        kern,
        grid=(t // TM, n_conv + 1),
        in_specs=[
            pl.BlockSpec((TM, d), lambda i, c: (i, 0)),
            pl.BlockSpec((HALO, d), lambda i, c: (jnp.maximum(i * halo_blocks - 1, 0), 0)),
            pl.BlockSpec((1, d), lambda i, c: (0, 0)),
            pl.BlockSpec((d, tn), lambda i, c: (0, c)),
            pl.BlockSpec((CONV_SIZE, CONV_CW), lambda i, c: (0, jnp.minimum(c, last))),
            pl.BlockSpec((1, CONV_CW), lambda i, c: (0, jnp.minimum(c, last))),
            pl.BlockSpec((MEM_TOKENS, MEM_WIDTH), lambda i, c: (i // tiles_per_seq, 0)),
            pl.BlockSpec((MEM_TOKENS, MEM_WIDTH), lambda i, c: (i // tiles_per_seq, 0)),
            pl.BlockSpec((1, HEAD_DIM), lambda i, c: (0, 0)),
        ],
        out_specs=[
            pl.BlockSpec((TM, CONV_CW), lambda i, c: (i, jnp.minimum(c, last))),
            pl.BlockSpec((TM, MEM_WIDTH), lambda i, c: (i, 0)),
        ],
        out_shape=[jax.ShapeDtypeStruct((t, conv_width), BF16),
                   jax.ShapeDtypeStruct((t, MEM_WIDTH), BF16)],
        scratch_shapes=[pltpu.VMEM((TM + HALO, d), BF16)],
        compiler_params=_params(56, 2),
        name="a_in",
    )(x2d, x2d, norm[None, :], w_perm, conv_w, conv_b[None, :], mk, mv, q_gain[None, :])


def _out_proj_kernel(res_ref, y1_ref, y2_ref, w1_ref, w2_ref, o_ref):
    o_ref[...] = res_ref[...] + _dot(y1_ref[...], w1_ref[...]) + _dot(y2_ref[...], w2_ref[...])


def _out_proj(res, y1, y2, w1, w2):
    t, d = res.shape
    tn = 1024
    n1, n2 = y1.shape[1], y2.shape[1]
    return pl.pallas_call(
        _out_proj_kernel,
        grid=(t // TM, d // tn),
        in_specs=[
            pl.BlockSpec((TM, tn), lambda i, j: (i, j)),
            pl.BlockSpec((TM, n1), lambda i, j: (i, 0)),
            pl.BlockSpec((TM, n2), lambda i, j: (i, 0)),
            pl.BlockSpec((n1, tn), lambda i, j: (0, j)),
            pl.BlockSpec((n2, tn), lambda i, j: (0, j)),
        ],
        out_specs=pl.BlockSpec((TM, tn), lambda i, j: (i, j)),
        out_shape=jax.ShapeDtypeStruct((t, d), F32),
        compiler_params=_params(48, 2),
        name="out_proj",
    )(res, y1, y2, w1, w2)


def _store_transposed(dst_ref, val):
    vt = val.T.astype(BF16)
    for g in range(NSA_KV_HEADS):
        for cc in range(val.shape[0] // KC):
            dst_ref[0, g, cc] = vt[g * HEAD_DIM:(g + 1) * HEAD_DIM, cc * KC:(cc + 1) * KC]


def _nsa_kv_kernel(x_ref, g_ref, w_ref, kns_ref, knw_ref,
                   kc_ref, vc_ref, ks_ref, vst_ref, kw_ref, vwt_ref, h_ref):
    j = pl.program_id(1)

    @pl.when(j == 0)
    def _():
        h_ref[...] = _rms_rows(x_ref[...], g_ref[...]).astype(BF16)

    def normed_keys(acc, gain):
        return jnp.concatenate(
            [_rms_rows(acc[:, g * HEAD_DIM:(g + 1) * HEAD_DIM], gain).astype(BF16)
             for g in range(NSA_KV_HEADS)], axis=1)

    @pl.when(j == 0)
    def _():
        kc_ref[...] = _dot(h_ref[...], w_ref[...])

    @pl.when(j == 1)
    def _():
        vc_ref[...] = _dot(h_ref[...], w_ref[...])

    @pl.when(j == 2)
    def _():
        ks_ref[...] = normed_keys(_dot(h_ref[...], w_ref[...]), kns_ref[...])

    @pl.when(j == 3)
    def _():
        _store_transposed(vst_ref, _dot(h_ref[...], w_ref[...]))

    @pl.when(j == 4)
    def _():
        kw_ref[...] = normed_keys(_dot(h_ref[...], w_ref[...]), knw_ref[...])

    @pl.when(j == 5)
    def _():
        _store_transposed(vwt_ref, _dot(h_ref[...], w_ref[...]))


def _nsa_kv(x2d, norm, w, kn_slc, kn_win, batch, seq):
    t, d = x2d.shape
    bw = KV_BRANCH_WIDTH
    tiles_per_seq = seq // TM
    row = lambda i, j: (i, 0)
    vt_spec = pl.BlockSpec((1, NSA_KV_HEADS, TM // KC, HEAD_DIM, KC),
                           lambda i, j: (i // tiles_per_seq, 0, i % tiles_per_seq, 0, 0))
    vt_shape = jax.ShapeDtypeStruct((batch, NSA_KV_HEADS, seq // KC, HEAD_DIM, KC), BF16)
    return pl.pallas_call(
        _nsa_kv_kernel,
        grid=(t // TM, 6),
        in_specs=[
            pl.BlockSpec((TM, d), row),
            pl.BlockSpec((1, d), lambda i, j: (0, 0)),
            pl.BlockSpec((d, bw), lambda i, j: (0, j)),
            pl.BlockSpec((1, HEAD_DIM), lambda i, j: (0, 0)),
            pl.BlockSpec((1, HEAD_DIM), lambda i, j: (0, 0)),
        ],
        out_specs=[pl.BlockSpec((TM, bw), row), pl.BlockSpec((TM, bw), row),
                   pl.BlockSpec((TM, bw), row), vt_spec,
                   pl.BlockSpec((TM, bw), row), vt_spec],
        out_shape=[jax.ShapeDtypeStruct((t, bw), F32), jax.ShapeDtypeStruct((t, bw), F32),
                   jax.ShapeDtypeStruct((t, bw), BF16), vt_shape,
                   jax.ShapeDtypeStruct((t, bw), BF16), vt_shape],
        scratch_shapes=[pltpu.VMEM((TM, d), BF16)],
        compiler_params=_params(48, 2),
        name="nsa_kv",
    )(x2d, norm[None, :], w, kn_slc[None, :], kn_win[None, :])


def _compress_kernel(x_ref, pa_ref, pb_ref, w1a_ref, w1b_ref, w2_ref, kn_ref, o_ref,
                     *, is_key, segs):
    x = x_ref[...]
    rows = x.shape[0]
    n_chunks = rows // segs
    a = _dot((x + pa_ref[...]).astype(BF16), w1a_ref[...])
    b = _dot((x + pb_ref[...]).astype(BF16), w1b_ref[...])
    hid = _silu(a + pltpu.roll(b, rows - 1, axis=0))
    o = _dot(hid.astype(BF16), w2_ref[...])
    if is_key:
        o = _rms_rows(o, kn_ref[...])
    chunk = lax.broadcasted_iota(jnp.int32, o.shape, 0) & (n_chunks - 1)
    o = jnp.where(chunk == n_chunks - 1, 0.0, o)
    if is_key:
        o_ref[...] = o.astype(BF16)
    else:
        for s in range(segs):
            o_ref[s] = o[s * n_chunks:(s + 1) * n_chunks, :].T.astype(BF16)


def _compress(xc, pos, w1, w2, kn, is_key, n_chunks):
    rows, width = xc.shape
    segs = min(8, rows // n_chunks)
    tr = segs * n_chunks
    half = CMP_STRIDE
    pa = pos[:half].reshape(1, width)
    pb = pos[half:].reshape(1, width)
    w1a = w1[:half].reshape(width, CMP_HIDDEN).astype(BF16)
    w1b = w1[half:].reshape(width, CMP_HIDDEN).astype(BF16)
    kern = functools.partial(_compress_kernel, is_key=is_key, segs=segs)
    const = lambda i: (0, 0)
    if is_key:
        out_spec = pl.BlockSpec((tr, HEAD_DIM), lambda i: (i, 0))
        out_shape = jax.ShapeDtypeStruct((rows, HEAD_DIM), BF16)
    else:
        out_spec = pl.BlockSpec((segs, HEAD_DIM, n_chunks), lambda i: (i, 0, 0))
        out_shape = jax.ShapeDtypeStruct((rows // n_chunks, HEAD_DIM, n_chunks), BF16)
    return pl.pallas_call(
        kern,
        grid=(rows // tr,),
        in_specs=[
            pl.BlockSpec((tr, width), lambda i: (i, 0)),
            pl.BlockSpec((1, width), const), pl.BlockSpec((1, width), const),
            pl.BlockSpec((width, CMP_HIDDEN), const), pl.BlockSpec((width, CMP_HIDDEN), const),
            pl.BlockSpec((CMP_HIDDEN, HEAD_DIM), const),
            pl.BlockSpec((1, HEAD_DIM), const),
        ],
        out_specs=out_spec,
        out_shape=out_shape,
        compiler_params=_params(40, 1),
        name="compress_k" if is_key else "compress_v",
    )(xc, pa, pb, w1a, w1b, w2.astype(BF16), kn[None, :])


def _b_in_kernel(x_ref, g_ref, w_ref, wg_ref, gb_ref, qg_ref, mk_ref, mv_ref, mqg_ref,
                 qt_ref, gate_ref, sz_ref, ym_ref, h_ref, om_ref):
    j = pl.program_id(1)
    tm = x_ref.shape[0]
    heads_per_step = w_ref.shape[1] // HEAD_DIM

    @pl.when(j == 0)
    def _():
        h_ref[...] = _rms_rows(x_ref[...], g_ref[...]).astype(BF16)
        gate_ref[...] = jax.nn.sigmoid(_dot(h_ref[...], wg_ref[...]) + gb_ref[...])

    @pl.when(j < 3)
    def _():
        acc = _dot(h_ref[...], w_ref[...])
        gain = jnp.tile(qg_ref[...], (1, tm // HEAD_DIM))
        for hd in range(heads_per_step):
            qt = acc[:, hd * HEAD_DIM:(hd + 1) * HEAD_DIM].T
            inv = lax.rsqrt(jnp.mean(qt * qt, axis=0, keepdims=True) + EPS)
            qt_ref[0, hd * HEAD_DIM:(hd + 1) * HEAD_DIM, :] = (qt * inv * gain).astype(BF16)

    @pl.when((j >= 3) & (j < 6))
    def _():
        sz_ref[...] = _silu(_dot(h_ref[...], w_ref[...])).astype(BF16)

    @pl.when(j == 6)
    def _():
        outs = _mem_attention(_dot(h_ref[...], w_ref[...]), mk_ref, mv_ref, mqg_ref[...])
        for hd in range(MEM_HEADS):
            om_ref[:, hd * HEAD_DIM:(hd + 1) * HEAD_DIM] = outs[hd]

    @pl.when(j == 7)
    def _():
        ym_ref[...] = (om_ref[...] * _silu(_dot(h_ref[...], w_ref[...]))).astype(BF16)


def _b_in(x2d, norm, w_main, w_gate, gate_bias, q_gain_b, mk, mv, mem_q_gain, batch, seq):
    t, d = x2d.shape
    tn = 512
    tiles_per_seq = seq // TM
    const = lambda i, j: (0, 0)
    mem_idx = lambda i, j: (i // tiles_per_seq, 0)
    return pl.pallas_call(
        _b_in_kernel,
        grid=(t // TM, 8),
        in_specs=[
            pl.BlockSpec((TM, d), lambda i, j: (i, 0)),
            pl.BlockSpec((1, d), const),
            pl.BlockSpec((d, tn), lambda i, j: (0, j)),
            pl.BlockSpec((d, V7X_LANES), const),
            pl.BlockSpec((1, V7X_LANES), const),
            pl.BlockSpec((HEAD_DIM, HEAD_DIM), const),
            pl.BlockSpec((MEM_TOKENS, MEM_WIDTH), mem_idx),
            pl.BlockSpec((MEM_TOKENS, MEM_WIDTH), mem_idx),
            pl.BlockSpec((1, HEAD_DIM), const),
        ],
        out_specs=[
            pl.BlockSpec((1, tn, TM), lambda i, j: (i // tiles_per_seq, jnp.minimum(j, 2),
                                                     i % tiles_per_seq)),
            pl.BlockSpec((TM, V7X_LANES), lambda i, j: (i, 0)),
            pl.BlockSpec((TM, tn), lambda i, j: (i, jnp.clip(j - 3, 0, 2))),
            pl.BlockSpec((TM, MEM_WIDTH), lambda i, j: (i, 0)),
        ],
        out_shape=[jax.ShapeDtypeStruct((batch, NSA_WIDTH, seq), BF16),
                   jax.ShapeDtypeStruct((t, V7X_LANES), F32),
                   jax.ShapeDtypeStruct((t, NSA_WIDTH), BF16),
                   jax.ShapeDtypeStruct((t, MEM_WIDTH), BF16)],
        scratch_shapes=[pltpu.VMEM((TM, d), BF16), pltpu.VMEM((TM, MEM_WIDTH), F32)],
        compiler_params=_params(48, 2),
        name="b_in",
    )(x2d, norm[None, :], w_main, w_gate, gate_bias, q_gain_b, mk, mv, mem_q_gain[None, :])


def _attn_block(qi, q3, gates, kcmp_ref, vcmpt_ref, ks_ref, vst_ref, kw_ref, vwt_ref,
                sz_ref, cover_ref, onehot_ref, y_ref, ssel_ref, swin_ref):
    n = NSA_GROUP * TQ
    qs = qi * TQ
    q = [q3[j * HEAD_DIM:(j + 1) * HEAD_DIM] for j in range(NSA_GROUP)]
    k_loc = lax.broadcasted_iota(jnp.int32, (KC, TQ), 0)
    t_loc = lax.broadcasted_iota(jnp.int32, (KC, TQ), 1)
    causal = k_loc <= t_loc
    band = k_loc > t_loc
    sel_chunks = range(qi + 1)
    win_chunks = range(max(qi - WINDOW // KC, 0), qi + 1)
    n_allowed = (qs + TQ - 1) // SLC_BLOCK + 1
    use_bias = n_allowed > SLC_TOPK

    def scores_pass(is_sel, j, sel_bias=None):
        qj = q[j]
        k_ref, s_ref = (ks_ref, ssel_ref) if is_sel else (kw_ref, swin_ref)
        if sel_bias is not None:
            pad = jnp.zeros((HEAD_DIM - sel_bias.shape[0], TQ), BF16)
            qj = jnp.concatenate([qj, sel_bias, pad], axis=0)
        m = None
        for i, ck in enumerate(sel_chunks if is_sel else win_chunks):
            k = k_ref[ck * KC:(ck + 1) * KC, :]
            if sel_bias is not None:
                k = jnp.concatenate([k, onehot_ref[ck * KC:(ck + 1) * KC, :]], axis=1)
            sc = _dot(k, qj)
            if not is_sel and ck == qi - WINDOW // KC:
                sc = jnp.where(band, sc, NEG)
            if ck == qi:
                sc = jnp.where(causal, sc, NEG)
            s_ref[j, i * KC:(i + 1) * KC, :] = sc
            m_c = jnp.max(sc, axis=0, keepdims=True)
            m = m_c if m is None else jnp.maximum(m, m_c)
        return m

    def values_pass(is_sel, j, m):
        vt_ref, s_ref = (vst_ref, ssel_ref) if is_sel else (vwt_ref, swin_ref)
        l = acc = None
        for i, ck in enumerate(sel_chunks if is_sel else win_chunks):
            pe = jnp.exp2(s_ref[j, i * KC:(i + 1) * KC, :] - m)
            l_c = jnp.sum(pe, axis=0, keepdims=True)
            pv = _dot(vt_ref[0, 0, ck], pe.astype(BF16))
            l = l_c if l is None else l + l_c
            acc = pv if acc is None else acc + pv
        return acc / l

    def merge(j, o_cmp_j, o_sel_j, o_win_j):
        yt = (gates[3 * j:3 * j + 1] * o_cmp_j + gates[3 * j + 1:3 * j + 2] * o_sel_j
              + gates[3 * j + 2:3 * j + 3] * o_win_j)
        ch = slice(j * HEAD_DIM, (j + 1) * HEAD_DIM)
        y_ref[:, ch] = (yt.T * sz_ref[:, ch].astype(F32)).astype(BF16)

    n_cmp = kcmp_ref.shape[0]
    s_cmp = _dot(kcmp_ref[...], jnp.concatenate(q, axis=1))
    m_win = [scores_pass(False, 0)]

    c_idx = lax.broadcasted_iota(jnp.int32, (n_cmp, n), 0)
    t_idx = qs + (lax.broadcasted_iota(jnp.int32, (n_cmp, n), 1) & (TQ - 1))
    cmask = c_idx * CMP_STRIDE + (CMP_BLOCK - 1) <= t_idx
    sm = jnp.where(cmask, s_cmp, NEG)
    e = jnp.exp2(sm - jnp.max(sm, axis=0, keepdims=True))
    p = jnp.where(cmask, e / jnp.sum(e, axis=0, keepdims=True), 0.0)
    o_cmp = _dot(vcmpt_ref[0], p.astype(BF16))
    m_win.append(scores_pass(False, 1))

    sel_bias = None
    if use_bias:
        imp = p[:, 0:TQ] + p[:, TQ:2 * TQ] + p[:, 2 * TQ:3 * TQ]
        hi = imp.astype(BF16)
        r1 = imp - hi.astype(F32)
        mid = r1.astype(BF16)
        lo = (r1 - mid.astype(F32)).astype(BF16)
        cover = cover_ref[...]
        score = _dot(cover, hi) + _dot(cover, mid) + _dot(cover, lo)
        n_sel = score.shape[0]
        j_idx = lax.broadcasted_iota(jnp.int32, (n_sel, TQ), 0)
        tq_idx = qs + lax.broadcasted_iota(jnp.int32, (n_sel, TQ), 1)
        cur = tq_idx >> SLC_SHIFT
        allowed = j_idx * SLC_BLOCK <= tq_idx
        forced = (j_idx == 0) | (j_idx == cur) | (j_idx == cur - 1)
        score = jnp.where(forced, FORCE, jnp.where(allowed, score, NEG))
        rank = jnp.zeros((n_sel, TQ), jnp.int32)
        for jp in range(n_allowed):
            row = score[jp:jp + 1, :]
            before = (row > score) | ((row == score) & (j_idx > jp))
            rank = rank + before.astype(jnp.int32)
        sel_bias = jnp.where(allowed & (rank < SLC_TOPK), 0.0, NEG).astype(BF16)

    o_win = [values_pass(False, 0, m_win[0])]
    m_win.append(scores_pass(False, 2))
    o_win.append(values_pass(False, 1, m_win[1]))
    m_sel = [scores_pass(True, 0, sel_bias)]
    o_win.append(values_pass(False, 2, m_win[2]))
    for j in range(NSA_GROUP):
        if j + 1 < NSA_GROUP:
            m_sel.append(scores_pass(True, j + 1, sel_bias))
        merge(j, o_cmp[:, j * TQ:(j + 1) * TQ], values_pass(True, j, m_sel[j]), o_win[j])


def _nsa_attn_kernel(qt_ref, kcmp_ref, vcmpt_ref, ks_ref, vst_ref, kw_ref, vwt_ref,
                     gate_ref, sz_ref, cover_ref, onehot_ref, y_ref, gt_ref, ssel_ref, swin_ref):
    g = pl.program_id(1)
    qi = pl.program_id(2)
    q3 = qt_ref[0]
    gt_ref[...] = gate_ref[...].T
    gates = gt_ref[pl.ds(pl.multiple_of(g * GATE_LANES, GATE_LANES), GATE_LANES), :]

    for q in range(ks_ref.shape[0] // TQ):
        pl.when(qi == q)(functools.partial(
            _attn_block, q, q3, gates, kcmp_ref, vcmpt_ref, ks_ref, vst_ref, kw_ref, vwt_ref,
            sz_ref, cover_ref, onehot_ref, y_ref, ssel_ref, swin_ref))


def _nsa_attn(qt, kcmp, vcmpt, ks, vst, kw, vwt, gates, sz, batch, seq):
    n_q = seq // TQ
    n_cmp = seq // CMP_STRIDE
    n_sel = seq // SLC_BLOCK
    n = NSA_GROUP * TQ
    i = np.arange(n_cmp)[:, None]
    j = np.arange(n_sel)[None, :]
    cover = ((i * CMP_STRIDE < (j + 1) * SLC_BLOCK) & (i * CMP_STRIDE + CMP_BLOCK > j * SLC_BLOCK))
    cover_t = jnp.asarray(cover.T, BF16)
    onehot = jnp.asarray(np.arange(seq)[:, None] // SLC_BLOCK == np.arange(HEAD_DIM)[None, :], BF16)
    kv_spec = pl.BlockSpec((seq, HEAD_DIM), lambda b, g, q: (b, g))
    vt_spec = pl.BlockSpec((1, 1, seq // KC, HEAD_DIM, KC), lambda b, g, q: (b, g, 0, 0, 0))
    gw = NSA_GROUP * HEAD_DIM
    return pl.pallas_call(
        _nsa_attn_kernel,
        grid=(batch, NSA_KV_HEADS, n_q),
        in_specs=[
            pl.BlockSpec((1, gw, TQ), lambda b, g, q: (b, g, q)),
            pl.BlockSpec((n_cmp, HEAD_DIM), lambda b, g, q: (b * NSA_KV_HEADS + g, 0)),
            pl.BlockSpec((1, HEAD_DIM, n_cmp), lambda b, g, q: (b * NSA_KV_HEADS + g, 0, 0)),
            kv_spec, vt_spec, kv_spec, vt_spec,
            pl.BlockSpec((TQ, V7X_LANES), lambda b, g, q: (b * n_q + q, 0)),
            pl.BlockSpec((TQ, gw), lambda b, g, q: (b * n_q + q, g)),
            pl.BlockSpec((n_sel, n_cmp), lambda b, g, q: (0, 0)),
            pl.BlockSpec((seq, HEAD_DIM), lambda b, g, q: (0, 0)),
        ],
        out_specs=pl.BlockSpec((TQ, gw), lambda b, g, q: (b * n_q + q, g)),
        out_shape=jax.ShapeDtypeStruct((batch * seq, NSA_WIDTH), BF16),
        scratch_shapes=[pltpu.VMEM((V7X_LANES, TQ), F32),
                        pltpu.VMEM((NSA_GROUP, seq, TQ), F32),
                        pltpu.VMEM((NSA_GROUP, WINDOW + KC, TQ), F32)],
        compiler_params=_params(40, 3),
        name="nsa_attn",
    )(qt, kcmp, vcmpt, ks, vst, kw, vwt, gates, sz, cover_t, onehot)


def _permute_a_w_in(w, conv_width):
    n_conv = conv_width // CONV_CW
    parts = [w[:, k * conv_width:(k + 1) * conv_width].reshape(-1, n_conv, CONV_CW)
             for k in range(4)]
    conv = jnp.stack(parts, axis=2).reshape(w.shape[0], 4 * conv_width)
    return jnp.concatenate([conv, w[:, 4 * conv_width:]], axis=1).astype(BF16)


def _gate_layout(v):
    per_group = NSA_GROUP * 3
    v = v.reshape(v.shape[:-1] + (NSA_KV_HEADS, per_group))
    v = jnp.pad(v, [(0, 0)] * (v.ndim - 1) + [(0, GATE_LANES - per_group)])
    return v.reshape(v.shape[:-2] + (NSA_KV_HEADS * GATE_LANES,))


def kernel(x, mem, a_norm, a_w_in, a_conv_w, a_conv_b, a_w_out, kv_norm, kv_w,
           cmp_pos_k, cmp_w1_k, cmp_w2_k, cmp_pos_v, cmp_w1_v, cmp_w2_v,
           kn_cmp, kn_slc, kn_win, b_norm, b_w_in, b_gate_bias, b_q_norm, b_w_out,
           mem_norm, mem_w_kv, mem_q_norm, mem_k_norm):
    batch, seq, d = x.shape
    conv_width = d - MEM_WIDTH
    assert a_norm.shape[0] == 1 and b_norm.shape[0] == 1 and seq % TM == 0
    x2d = x.reshape(batch * seq, d)
    mem2d = mem.reshape(batch * MEM_TOKENS, d)

    mk, mv = _mem_kv(mem2d, mem_norm, mem_w_kv.astype(BF16), mem_k_norm)

    y_conv, y_mem = _a_in(x2d, a_norm[0], _permute_a_w_in(a_w_in[0], conv_width), a_conv_w[0],
                          a_conv_b[0], mk[0], mv[0], mem_q_norm[0], seq)
    w_out = a_w_out[0].astype(BF16)
    x1 = _out_proj(x2d, y_conv, y_mem, w_out[:conv_width], w_out[conv_width:])

    kc, vc, ks, vst, kw, vwt = _nsa_kv(x1, kv_norm, kv_w.astype(BF16), kn_slc, kn_win, batch, seq)
    n_chunks = seq // CMP_STRIDE

    def chunked(t):
        t = t.reshape(batch, n_chunks, CMP_STRIDE, NSA_KV_HEADS, HEAD_DIM)
        return t.transpose(0, 3, 1, 2, 4).reshape(batch * NSA_KV_HEADS * n_chunks,
                                                  CMP_STRIDE * HEAD_DIM)

    k_cmp = _compress(chunked(kc), cmp_pos_k, cmp_w1_k, cmp_w2_k, kn_cmp, True, n_chunks)
    v_cmp_t = _compress(chunked(vc), cmp_pos_v, cmp_w1_v, cmp_w2_v, kn_cmp, False, n_chunks)

    wb = b_w_in[0]
    o = 0
    w_q = wb[:, o:o + NSA_WIDTH]; o += NSA_WIDTH
    w_g = wb[:, o:o + 3 * NSA_HEADS]; o += 3 * NSA_HEADS
    w_z = wb[:, o:o + NSA_WIDTH]; o += NSA_WIDTH
    w_m = wb[:, o:]
    w_main = jnp.concatenate([w_q, w_z, w_m], axis=1).astype(BF16)
    q_gain_b = jnp.broadcast_to((b_q_norm[0] * (SCALE * LOG2E))[:, None], (HEAD_DIM, HEAD_DIM))
    qt, gates, sz, y_mem2 = _b_in(x1, b_norm[0], w_main, _gate_layout(w_g).astype(BF16),
                                  _gate_layout(b_gate_bias[0])[None, :], q_gain_b,
                                  mk[1], mv[1], mem_q_norm[1], batch, seq)
    y_nsa = _nsa_attn(qt, k_cmp, v_cmp_t, ks, vst, kw, vwt, gates, sz, batch, seq)
    w_out2 = b_w_out[0].astype(BF16)
    out = _out_proj(x1, y_nsa, y_mem2, w_out2[:NSA_WIDTH], w_out2[NSA_WIDTH:])
    return out.reshape(batch, seq, d)
```

```python
import functools

import numpy as np
import jax
import jax.numpy as jnp
from jax import lax
from jax.experimental import pallas as pl
from jax.experimental.pallas import tpu as pltpu

HEAD_DIM = 128
MEM_TOKENS = 256
MEM_HEADS = 4
MEM_WIDTH = MEM_HEADS * HEAD_DIM
NSA_KV_HEADS = 4
NSA_GROUP = 3
NSA_HEADS = NSA_KV_HEADS * NSA_GROUP
NSA_WIDTH = NSA_HEADS * HEAD_DIM
KV_BRANCH_WIDTH = NSA_KV_HEADS * HEAD_DIM
CONV_SIZE = 3
CMP_BLOCK = 32
CMP_STRIDE = 16
CMP_HIDDEN = 256
SLC_BLOCK = 64
SLC_SHIFT = 6
SLC_TOPK = 16
WINDOW = 512
EPS = 1e-6
NEG = -1e30
FORCE = 1e4
SCALE = HEAD_DIM ** -0.5
LOG2E = 1.4426950408889634

V7X_VMEM_BYTES = 64 * 1024 * 1024
V7X_LANES = 128
BF16_SUBLANES = 16

TM = 1024
HALO = BF16_SUBLANES
CONV_CW = 256
TQ = 256
KC = 256
GATE_LANES = 32

BF16 = jnp.bfloat16
F32 = jnp.float32


def _params(vmem_mb, n_grid):
    return pltpu.CompilerParams(
        dimension_semantics=("arbitrary",) * n_grid,
        vmem_limit_bytes=vmem_mb * 1024 * 1024)


def _rms_rows(v, gain):
    return v * lax.rsqrt(jnp.mean(v * v, axis=-1, keepdims=True) + EPS) * gain


def _silu(v):
    return v * jax.nn.sigmoid(v)


def _dot(a, b):
    return jnp.dot(a, b, preferred_element_type=F32)


def _dot_nt(a, b):
    return lax.dot_general(a, b, (((1,), (1,)), ((), ())), preferred_element_type=F32)


def _mem_attention(mq, mk_ref, mv_ref, q_gain):
    outs = []
    for h in range(MEM_HEADS):
        sl = slice(h * HEAD_DIM, (h + 1) * HEAD_DIM)
        q = _rms_rows(mq[:, sl], q_gain * SCALE).astype(BF16)
        s = _dot_nt(q, mk_ref[:, sl])
        e = jnp.exp(s - jnp.max(s, axis=-1, keepdims=True))
        l = jnp.sum(e, axis=-1, keepdims=True)
        outs.append(_dot(e.astype(BF16), mv_ref[:, sl]) / l)
    return outs


def _mem_kv_kernel(mem_ref, g_ref, w_ref, kg_ref, mk_ref, mv_ref):
    h = _rms_rows(mem_ref[...], g_ref[0]).astype(BF16)
    kv = _dot(h, w_ref[0])
    for hd in range(MEM_HEADS):
        sl = slice(hd * HEAD_DIM, (hd + 1) * HEAD_DIM)
        mk_ref[0, :, sl] = _rms_rows(kv[:, sl], kg_ref[0]).astype(BF16)
    mv_ref[0] = kv[:, MEM_WIDTH:].astype(BF16)


def _mem_kv(mem2d, mem_norm, w_kv, k_gain):
    n_layers, d = mem_norm.shape
    rows = mem2d.shape[0]
    tm = min(TM, rows)
    out = jax.ShapeDtypeStruct((n_layers, rows, MEM_WIDTH), BF16)
    return pl.pallas_call(
        _mem_kv_kernel,
        grid=(n_layers, rows // tm),
        in_specs=[
            pl.BlockSpec((tm, d), lambda l, i: (i, 0)),
            pl.BlockSpec((1, 1, d), lambda l, i: (l, 0, 0)),
            pl.BlockSpec((1, d, 2 * MEM_WIDTH), lambda l, i: (l, 0, 0)),
            pl.BlockSpec((1, 1, HEAD_DIM), lambda l, i: (l, 0, 0)),
        ],
        out_specs=[pl.BlockSpec((1, tm, MEM_WIDTH), lambda l, i: (l, i, 0))] * 2,
        out_shape=[out, out],
        compiler_params=_params(40, 2),
        name="mem_kv",
    )(mem2d, mem_norm[:, None, :], w_kv, k_gain[:, None, :])


def _a_in_kernel(x_ref, xh_ref, g_ref, wb_ref, wc_ref, wh_ref, wz_ref, wm_ref, cw_ref, cb_ref,
                 mk_ref, mv_ref, qg_ref, yc_ref, ym_ref, h_ref, *, tiles_per_seq, n_conv):
    i = pl.program_id(0)
    c = pl.program_id(1)
    tm = x_ref.shape[0]

    @pl.when(c == 0)
    def _():
        keep = (i % tiles_per_seq != 0).astype(F32)
        h_ref[0:HALO, :] = (_rms_rows(xh_ref[...], g_ref[...]) * keep).astype(BF16)
        h_ref[HALO:, :] = _rms_rows(x_ref[...], g_ref[...]).astype(BF16)

    @pl.when(c < n_conv)
    def _():
        u = _dot(h_ref[...], wc_ref[...]) * _dot(h_ref[...], wh_ref[...])
        taps = cw_ref[...]
        y = (taps[0:1] * pltpu.roll(u, 2, axis=0)[HALO:]
             + taps[1:2] * pltpu.roll(u, 1, axis=0)[HALO:]
             + taps[2:3] * u[HALO:])
        h = h_ref[HALO:, :]
        yc_ref[...] = (_dot(h, wb_ref[...]) * (y + cb_ref[...])
                       * _silu(_dot(h, wz_ref[...]))).astype(BF16)

    @pl.when(c == n_conv)
    def _():
        acc = _dot(h_ref[HALO:, :], wm_ref[...])
        outs = _mem_attention(acc[:, :MEM_WIDTH], mk_ref, mv_ref, qg_ref[...])
        for hd in range(MEM_HEADS):
            sl = slice(hd * HEAD_DIM, (hd + 1) * HEAD_DIM)
            ym_ref[:, sl] = (outs[hd] * _silu(acc[:, MEM_WIDTH + hd * HEAD_DIM:
                                                  MEM_WIDTH + (hd + 1) * HEAD_DIM])).astype(BF16)


def _a_in(x2d, norm, w, conv_w, conv_b, mk, mv, q_gain, seq):
    t, d = x2d.shape
    conv_width = conv_w.shape[1]
    n_conv = conv_width // CONV_CW
    mem_cols = 2 * MEM_WIDTH
    assert w.shape[1] == 4 * conv_width + mem_cols and (4 * conv_width) % mem_cols == 0
    tiles_per_seq = seq // TM
    halo_blocks = TM // HALO
    kern = functools.partial(_a_in_kernel, tiles_per_seq=tiles_per_seq, n_conv=n_conv)
    last = n_conv - 1

    def conv_cols(part):
        return pl.BlockSpec((d, CONV_CW), lambda i, c: (0, part * n_conv + jnp.minimum(c, last)))

    return pl.pallas_call(
        kern,
        grid=(t // TM, n_conv + 1),
        in_specs=[
            pl.BlockSpec((TM, d), lambda i, c: (i, 0)),
            pl.BlockSpec((HALO, d), lambda i, c: (jnp.maximum(i * halo_blocks - 1, 0), 0)),
            pl.BlockSpec((1, d), lambda i, c: (0, 0)),
            conv_cols(0), conv_cols(1), conv_cols(2), conv_cols(3),
            pl.BlockSpec((d, mem_cols), lambda i, c: (0, 4 * conv_width // mem_cols)),
            pl.BlockSpec((CONV_SIZE, CONV_CW), lambda i, c: (0, jnp.minimum(c, last))),
            pl.BlockSpec((1, CONV_CW), lambda i, c: (0, jnp.minimum(c, last))),
            pl.BlockSpec((MEM_TOKENS, MEM_WIDTH), lambda i, c: (i // tiles_per_seq, 0)),
            pl.BlockSpec((MEM_TOKENS, MEM_WIDTH), lambda i, c: (i // tiles_per_seq, 0)),
            pl.BlockSpec((1, HEAD_DIM), lambda i, c: (0, 0)),
        ],
        out_specs=[
            pl.BlockSpec((TM, CONV_CW), lambda i, c: (i, jnp.minimum(c, last))),
            pl.BlockSpec((TM, MEM_WIDTH), lambda i, c: (i, 0)),
        ],
        out_shape=[jax.ShapeDtypeStruct((t, conv_width), BF16),
                   jax.ShapeDtypeStruct((t, MEM_WIDTH), BF16)],
        scratch_shapes=[pltpu.VMEM((TM + HALO, d), BF16)],
        compiler_params=_params(56, 2),
        name="a_in",
    )(x2d, x2d, norm[None, :], w, w, w, w, w, conv_w, conv_b[None, :], mk, mv, q_gain[None, :])


def _out_proj_kernel(res_ref, y1_ref, y2_ref, w1_ref, w2_ref, o_ref):
    o_ref[...] = res_ref[...] + _dot(y1_ref[...], w1_ref[...]) + _dot(y2_ref[...], w2_ref[...])


def _out_proj(res, y1, y2, w):
    t, d = res.shape
    tn = 1024
    n1, n2 = y1.shape[1], y2.shape[1]
    assert n1 % n2 == 0 and n1 + n2 == w.shape[0]
    return pl.pallas_call(
        _out_proj_kernel,
        grid=(t // TM, d // tn),
        in_specs=[
            pl.BlockSpec((TM, tn), lambda i, j: (i, j)),
            pl.BlockSpec((TM, n1), lambda i, j: (i, 0)),
            pl.BlockSpec((TM, n2), lambda i, j: (i, 0)),
            pl.BlockSpec((n1, tn), lambda i, j: (0, j)),
            pl.BlockSpec((n2, tn), lambda i, j: (n1 // n2, j)),
        ],
        out_specs=pl.BlockSpec((TM, tn), lambda i, j: (i, j)),
        out_shape=jax.ShapeDtypeStruct((t, d), F32),
        compiler_params=_params(48, 2),
        name="out_proj",
    )(res, y1, y2, w, w)


def _store_transposed(dst_ref, val):
    vt = val.T.astype(BF16)
    for g in range(NSA_KV_HEADS):
        for cc in range(val.shape[0] // KC):
            dst_ref[0, g, cc] = vt[g * HEAD_DIM:(g + 1) * HEAD_DIM, cc * KC:(cc + 1) * KC]


def _nsa_kv_kernel(x_ref, g_ref, w_ref, kns_ref, knw_ref,
                   kc_ref, vc_ref, ks_ref, vst_ref, kw_ref, vwt_ref, h_ref):
    j = pl.program_id(1)

    @pl.when(j == 0)
    def _():
        h_ref[...] = _rms_rows(x_ref[...], g_ref[...]).astype(BF16)

    def normed_keys(acc, gain):
        return jnp.concatenate(
            [_rms_rows(acc[:, g * HEAD_DIM:(g + 1) * HEAD_DIM], gain).astype(BF16)
             for g in range(NSA_KV_HEADS)], axis=1)

    @pl.when(j == 0)
    def _():
        kc_ref[...] = _dot(h_ref[...], w_ref[...])

    @pl.when(j == 1)
    def _():
        vc_ref[...] = _dot(h_ref[...], w_ref[...])

    @pl.when(j == 2)
    def _():
        ks_ref[...] = normed_keys(_dot(h_ref[...], w_ref[...]), kns_ref[...])

    @pl.when(j == 3)
    def _():
        _store_transposed(vst_ref, _dot(h_ref[...], w_ref[...]))

    @pl.when(j == 4)
    def _():
        kw_ref[...] = normed_keys(_dot(h_ref[...], w_ref[...]), knw_ref[...])

    @pl.when(j == 5)
    def _():
        _store_transposed(vwt_ref, _dot(h_ref[...], w_ref[...]))


def _nsa_kv(x2d, norm, w, kn_slc, kn_win, batch, seq):
    t, d = x2d.shape
    bw = KV_BRANCH_WIDTH
    tiles_per_seq = seq // TM
    row = lambda i, j: (i, 0)
    vt_spec = pl.BlockSpec((1, NSA_KV_HEADS, TM // KC, HEAD_DIM, KC),
                           lambda i, j: (i // tiles_per_seq, 0, i % tiles_per_seq, 0, 0))
    vt_shape = jax.ShapeDtypeStruct((batch, NSA_KV_HEADS, seq // KC, HEAD_DIM, KC), BF16)
    return pl.pallas_call(
        _nsa_kv_kernel,
        grid=(t // TM, 6),
        in_specs=[
            pl.BlockSpec((TM, d), row),
            pl.BlockSpec((1, d), lambda i, j: (0, 0)),
            pl.BlockSpec((d, bw), lambda i, j: (0, j)),
            pl.BlockSpec((1, HEAD_DIM), lambda i, j: (0, 0)),
            pl.BlockSpec((1, HEAD_DIM), lambda i, j: (0, 0)),
        ],
        out_specs=[pl.BlockSpec((TM, bw), row), pl.BlockSpec((TM, bw), row),
                   pl.BlockSpec((TM, bw), row), vt_spec,
                   pl.BlockSpec((TM, bw), row), vt_spec],
        out_shape=[jax.ShapeDtypeStruct((t, bw), F32), jax.ShapeDtypeStruct((t, bw), F32),
                   jax.ShapeDtypeStruct((t, bw), BF16), vt_shape,
                   jax.ShapeDtypeStruct((t, bw), BF16), vt_shape],
        scratch_shapes=[pltpu.VMEM((TM, d), BF16)],
        compiler_params=_params(48, 2),
        name="nsa_kv",
    )(x2d, norm[None, :], w, kn_slc[None, :], kn_win[None, :])


def _compress_kernel(*refs, is_key):
    x_refs = refs[:NSA_KV_HEADS]
    pos_ref, w1_ref, w2_ref, kn_ref, o_ref = refs[NSA_KV_HEADS:]
    n_chunks = x_refs[0].shape[0] // CMP_STRIDE
    pairs = CMP_STRIDE // 2
    first = second = None
    for p in range(pairs):
        lhs_a, lhs_b = [], []
        for l in (2 * p, 2 * p + 1):
            xg = jnp.concatenate([x_ref[pl.ds(l, n_chunks, stride=CMP_STRIDE), :]
                                  for x_ref in x_refs], axis=0)
            lhs_a.append((xg + pos_ref[l:l + 1, :]).astype(BF16))
            lhs_b.append((xg + pos_ref[CMP_STRIDE + l:CMP_STRIDE + l + 1, :]).astype(BF16))
        da = _dot(jnp.concatenate(lhs_a, axis=1), w1_ref[p])
        db = _dot(jnp.concatenate(lhs_b, axis=1), w1_ref[pairs + p])
        first = da if first is None else first + da
        second = db if second is None else second + db
    rows = first.shape[0]
    hid = _silu(first + pltpu.roll(second, rows - 1, axis=0))
    o = _dot(hid.astype(BF16), w2_ref[...])
    if is_key:
        o = _rms_rows(o, kn_ref[...])
    chunk = lax.broadcasted_iota(jnp.int32, o.shape, 0) & (n_chunks - 1)
    o = jnp.where(chunk == n_chunks - 1, 0.0, o)
    if is_key:
        o_ref[...] = o.astype(BF16)
    else:
        for g in range(NSA_KV_HEADS):
            o_ref[g] = o[g * n_chunks:(g + 1) * n_chunks, :].T.astype(BF16)


def _compress(t2d, pos, w1, w2, kn, is_key, batch, seq):
    assert t2d.shape[1] == NSA_KV_HEADS * HEAD_DIM
    n_chunks = seq // CMP_STRIDE
    rows = NSA_KV_HEADS * n_chunks
    w1p = w1.reshape(CMP_BLOCK // 2, 2 * HEAD_DIM, CMP_HIDDEN).astype(BF16)
    kern = functools.partial(_compress_kernel, is_key=is_key)
    if is_key:
        out_spec = pl.BlockSpec((rows, HEAD_DIM), lambda b: (b, 0))
        out_shape = jax.ShapeDtypeStruct((batch * rows, HEAD_DIM), BF16)
    else:
        out_spec = pl.BlockSpec((NSA_KV_HEADS, HEAD_DIM, n_chunks), lambda b: (b, 0, 0))
        out_shape = jax.ShapeDtypeStruct((batch * NSA_KV_HEADS, HEAD_DIM, n_chunks), BF16)
    return pl.pallas_call(
        kern,
        grid=(batch,),
        in_specs=[pl.BlockSpec((seq, HEAD_DIM), functools.partial(lambda b, g: (b, g), g=g))
                  for g in range(NSA_KV_HEADS)] + [
            pl.BlockSpec((CMP_BLOCK, HEAD_DIM), lambda b: (0, 0)),
            pl.BlockSpec((CMP_BLOCK // 2, 2 * HEAD_DIM, CMP_HIDDEN), lambda b: (0, 0, 0)),
            pl.BlockSpec((CMP_HIDDEN, HEAD_DIM), lambda b: (0, 0)),
            pl.BlockSpec((1, HEAD_DIM), lambda b: (0, 0)),
        ],
        out_specs=out_spec,
        out_shape=out_shape,
        compiler_params=_params(32, 1),
        name="compress_k" if is_key else "compress_v",
    )(*([t2d] * NSA_KV_HEADS), pos, w1p, w2.astype(BF16), kn[None, :])


def _b_in_kernel(x_ref, g_ref, w_ref, wg_ref, gb_ref, qg_ref, mk_ref, mv_ref, mqg_ref,
                 qt_ref, gate_ref, sz_ref, ym_ref, h_ref, om_ref):
    j = pl.program_id(1)
    tm = x_ref.shape[0]
    heads_per_step = w_ref.shape[1] // HEAD_DIM

    @pl.when(j == 0)
    def _():
        h_ref[...] = _rms_rows(x_ref[...], g_ref[...]).astype(BF16)
        gate_ref[...] = jax.nn.sigmoid(_dot(h_ref[...], wg_ref[...]) + gb_ref[...])

    @pl.when(j < 3)
    def _():
        acc = _dot(h_ref[...], w_ref[...])
        gain = jnp.tile(qg_ref[...], (1, tm // HEAD_DIM))
        for hd in range(heads_per_step):
            qt = acc[:, hd * HEAD_DIM:(hd + 1) * HEAD_DIM].T
            inv = lax.rsqrt(jnp.mean(qt * qt, axis=0, keepdims=True) + EPS)
            qt_ref[0, hd * HEAD_DIM:(hd + 1) * HEAD_DIM, :] = (qt * inv * gain).astype(BF16)

    @pl.when((j >= 3) & (j < 6))
    def _():
        sz_ref[...] = _silu(_dot(h_ref[...], w_ref[...])).astype(BF16)

    @pl.when(j == 6)
    def _():
        outs = _mem_attention(_dot(h_ref[...], w_ref[...]), mk_ref, mv_ref, mqg_ref[...])
        for hd in range(MEM_HEADS):
            om_ref[:, hd * HEAD_DIM:(hd + 1) * HEAD_DIM] = outs[hd]

    @pl.when(j == 7)
    def _():
        ym_ref[...] = (om_ref[...] * _silu(_dot(h_ref[...], w_ref[...]))).astype(BF16)


def _b_in(x2d, norm, w_main, w_gate, gate_bias, q_gain_b, mk, mv, mem_q_gain, batch, seq):
    t, d = x2d.shape
    tn = 512
    tiles_per_seq = seq // TM
    const = lambda i, j: (0, 0)
    mem_idx = lambda i, j: (i // tiles_per_seq, 0)
    return pl.pallas_call(
        _b_in_kernel,
        grid=(t // TM, 8),
        in_specs=[
            pl.BlockSpec((TM, d), lambda i, j: (i, 0)),
            pl.BlockSpec((1, d), const),
            pl.BlockSpec((d, tn), lambda i, j: (0, j)),
            pl.BlockSpec((d, V7X_LANES), const),
            pl.BlockSpec((1, V7X_LANES), const),
            pl.BlockSpec((HEAD_DIM, HEAD_DIM), const),
            pl.BlockSpec((MEM_TOKENS, MEM_WIDTH), mem_idx),
            pl.BlockSpec((MEM_TOKENS, MEM_WIDTH), mem_idx),
            pl.BlockSpec((1, HEAD_DIM), const),
        ],
        out_specs=[
            pl.BlockSpec((1, tn, TM), lambda i, j: (i // tiles_per_seq, jnp.minimum(j, 2),
                                                     i % tiles_per_seq)),
            pl.BlockSpec((TM, V7X_LANES), lambda i, j: (i, 0)),
            pl.BlockSpec((TM, tn), lambda i, j: (i, jnp.clip(j - 3, 0, 2))),
            pl.BlockSpec((TM, MEM_WIDTH), lambda i, j: (i, 0)),
        ],
        out_shape=[jax.ShapeDtypeStruct((batch, NSA_WIDTH, seq), BF16),
                   jax.ShapeDtypeStruct((t, V7X_LANES), F32),
                   jax.ShapeDtypeStruct((t, NSA_WIDTH), BF16),
                   jax.ShapeDtypeStruct((t, MEM_WIDTH), BF16)],
        scratch_shapes=[pltpu.VMEM((TM, d), BF16), pltpu.VMEM((TM, MEM_WIDTH), F32)],
        compiler_params=_params(48, 2),
        name="b_in",
    )(x2d, norm[None, :], w_main, w_gate, gate_bias, q_gain_b, mk, mv, mem_q_gain[None, :])


def _attn_block(qi, q3, gates, kcmp_ref, vcmpt_ref, ks_ref, vst_ref, kw_ref, vwt_ref,
                sz_ref, cover_ref, onehot_ref, y_ref, ssel_ref, swin_ref):
    n = NSA_GROUP * TQ
    qs = qi * TQ
    q = [q3[j * HEAD_DIM:(j + 1) * HEAD_DIM] for j in range(NSA_GROUP)]
    k_loc = lax.broadcasted_iota(jnp.int32, (KC, TQ), 0)
    t_loc = lax.broadcasted_iota(jnp.int32, (KC, TQ), 1)
    causal = k_loc <= t_loc
    band = k_loc > t_loc
    sel_chunks = range(qi + 1)
    win_chunks = range(max(qi - WINDOW // KC, 0), qi + 1)
    n_allowed = (qs + TQ - 1) // SLC_BLOCK + 1
    use_bias = n_allowed > SLC_TOPK

    def scores_pass(is_sel, j, sel_bias=None):
        qj = q[j]
        k_ref, s_ref = (ks_ref, ssel_ref) if is_sel else (kw_ref, swin_ref)
        if sel_bias is not None:
            pad = jnp.zeros((HEAD_DIM - sel_bias.shape[0], TQ), BF16)
            qj = jnp.concatenate([qj, sel_bias, pad], axis=0)
        m = None
        for i, ck in enumerate(sel_chunks if is_sel else win_chunks):
            k = k_ref[ck * KC:(ck + 1) * KC, :]
            if sel_bias is not None:
                k = jnp.concatenate([k, onehot_ref[ck * KC:(ck + 1) * KC, :]], axis=1)
            sc = _dot(k, qj)
            if not is_sel and ck == qi - WINDOW // KC:
                sc = jnp.where(band, sc, NEG)
            if ck == qi:
                sc = jnp.where(causal, sc, NEG)
            s_ref[j, i * KC:(i + 1) * KC, :] = sc
            m_c = jnp.max(sc, axis=0, keepdims=True)
            m = m_c if m is None else jnp.maximum(m, m_c)
        return m

    def values_pass(is_sel, j, m):
        vt_ref, s_ref = (vst_ref, ssel_ref) if is_sel else (vwt_ref, swin_ref)
        l = acc = None
        for i, ck in enumerate(sel_chunks if is_sel else win_chunks):
            pe = jnp.exp2(s_ref[j, i * KC:(i + 1) * KC, :] - m)
            l_c = jnp.sum(pe, axis=0, keepdims=True)
            pv = _dot(vt_ref[0, 0, ck], pe.astype(BF16))
            l = l_c if l is None else l + l_c
            acc = pv if acc is None else acc + pv
        return acc / l

    def merge(j, o_cmp_j, o_sel_j, o_win_j):
        yt = (gates[3 * j:3 * j + 1] * o_cmp_j + gates[3 * j + 1:3 * j + 2] * o_sel_j
              + gates[3 * j + 2:3 * j + 3] * o_win_j)
        ch = slice(j * HEAD_DIM, (j + 1) * HEAD_DIM)
        y_ref[:, ch] = (yt.T * sz_ref[:, ch].astype(F32)).astype(BF16)

    n_cmp = kcmp_ref.shape[0]
    s_cmp = _dot(kcmp_ref[...], jnp.concatenate(q, axis=1))
    m_win = [scores_pass(False, 0)]

    c_idx = lax.broadcasted_iota(jnp.int32, (n_cmp, n), 0)
    t_idx = qs + (lax.broadcasted_iota(jnp.int32, (n_cmp, n), 1) & (TQ - 1))
    cmask = c_idx * CMP_STRIDE + (CMP_BLOCK - 1) <= t_idx
    sm = jnp.where(cmask, s_cmp, NEG)
    e = jnp.exp2(sm - jnp.max(sm, axis=0, keepdims=True))
    p = jnp.where(cmask, e / jnp.sum(e, axis=0, keepdims=True), 0.0)
    o_cmp = _dot(vcmpt_ref[0], p.astype(BF16))
    m_win.append(scores_pass(False, 1))

    sel_bias = None
    if use_bias:
        imp = p[:, 0:TQ] + p[:, TQ:2 * TQ] + p[:, 2 * TQ:3 * TQ]
        hi = imp.astype(BF16)
        r1 = imp - hi.astype(F32)
        mid = r1.astype(BF16)
        lo = (r1 - mid.astype(F32)).astype(BF16)
        cover = cover_ref[...]
        score = _dot(cover, hi) + _dot(cover, mid) + _dot(cover, lo)
        n_sel = score.shape[0]
        j_idx = lax.broadcasted_iota(jnp.int32, (n_sel, TQ), 0)
        tq_idx = qs + lax.broadcasted_iota(jnp.int32, (n_sel, TQ), 1)
        cur = tq_idx >> SLC_SHIFT
        allowed = j_idx * SLC_BLOCK <= tq_idx
        forced = (j_idx == 0) | (j_idx == cur) | (j_idx == cur - 1)
        score = jnp.where(forced, FORCE, jnp.where(allowed, score, NEG))
        rank = jnp.zeros((n_sel, TQ), jnp.int32)
        for jp in range(n_allowed):
            row = score[jp:jp + 1, :]
            before = (row > score) | ((row == score) & (j_idx > jp))
            rank = rank + before.astype(jnp.int32)
        sel_bias = jnp.where(allowed & (rank < SLC_TOPK), 0.0, NEG).astype(BF16)

    o_win = [values_pass(False, 0, m_win[0])]
    m_win.append(scores_pass(False, 2))
    o_win.append(values_pass(False, 1, m_win[1]))
    m_sel = [scores_pass(True, 0, sel_bias)]
    o_win.append(values_pass(False, 2, m_win[2]))
    for j in range(NSA_GROUP):
        if j + 1 < NSA_GROUP:
            m_sel.append(scores_pass(True, j + 1, sel_bias))
        merge(j, o_cmp[:, j * TQ:(j + 1) * TQ], values_pass(True, j, m_sel[j]), o_win[j])


def _nsa_attn_kernel(qt_ref, kcmp_ref, vcmpt_ref, ks_ref, vst_ref, kw_ref, vwt_ref,
                     gate_ref, sz_ref, cover_ref, onehot_ref, y_ref, gt_ref, ssel_ref, swin_ref):
    g = pl.program_id(1)
    qi = pl.program_id(2)
    q3 = qt_ref[0]
    gt_ref[...] = gate_ref[...].T
    gates = gt_ref[pl.ds(pl.multiple_of(g * GATE_LANES, GATE_LANES), GATE_LANES), :]

    for q in range(ks_ref.shape[0] // TQ):
        pl.when(qi == q)(functools.partial(
            _attn_block, q, q3, gates, kcmp_ref, vcmpt_ref, ks_ref, vst_ref, kw_ref, vwt_ref,
            sz_ref, cover_ref, onehot_ref, y_ref, ssel_ref, swin_ref))


def _nsa_attn(qt, kcmp, vcmpt, ks, vst, kw, vwt, gates, sz, batch, seq):
    n_q = seq // TQ
    n_cmp = seq // CMP_STRIDE
    n_sel = seq // SLC_BLOCK
    n = NSA_GROUP * TQ
    i = np.arange(n_cmp)[:, None]
    j = np.arange(n_sel)[None, :]
    cover = ((i * CMP_STRIDE < (j + 1) * SLC_BLOCK) & (i * CMP_STRIDE + CMP_BLOCK > j * SLC_BLOCK))
    cover_t = jnp.asarray(cover.T, BF16)
    onehot = jnp.asarray(np.arange(seq)[:, None] // SLC_BLOCK == np.arange(HEAD_DIM)[None, :], BF16)
    kv_spec = pl.BlockSpec((seq, HEAD_DIM), lambda b, g, q: (b, g))
    vt_spec = pl.BlockSpec((1, 1, seq // KC, HEAD_DIM, KC), lambda b, g, q: (b, g, 0, 0, 0))
    gw = NSA_GROUP * HEAD_DIM
    return pl.pallas_call(
        _nsa_attn_kernel,
        grid=(batch, NSA_KV_HEADS, n_q),
        in_specs=[
            pl.BlockSpec((1, gw, TQ), lambda b, g, q: (b, g, q)),
            pl.BlockSpec((n_cmp, HEAD_DIM), lambda b, g, q: (b * NSA_KV_HEADS + g, 0)),
            pl.BlockSpec((1, HEAD_DIM, n_cmp), lambda b, g, q: (b * NSA_KV_HEADS + g, 0, 0)),
            kv_spec, vt_spec, kv_spec, vt_spec,
            pl.BlockSpec((TQ, V7X_LANES), lambda b, g, q: (b * n_q + q, 0)),
            pl.BlockSpec((TQ, gw), lambda b, g, q: (b * n_q + q, g)),
            pl.BlockSpec((n_sel, n_cmp), lambda b, g, q: (0, 0)),
            pl.BlockSpec((seq, HEAD_DIM), lambda b, g, q: (0, 0)),
        ],
        out_specs=pl.BlockSpec((TQ, gw), lambda b, g, q: (b * n_q + q, g)),
        out_shape=jax.ShapeDtypeStruct((batch * seq, NSA_WIDTH), BF16),
        scratch_shapes=[pltpu.VMEM((V7X_LANES, TQ), F32),
                        pltpu.VMEM((NSA_GROUP, seq, TQ), F32),
                        pltpu.VMEM((NSA_GROUP, WINDOW + KC, TQ), F32)],
        compiler_params=_params(40, 3),
        name="nsa_attn",
    )(qt, kcmp, vcmpt, ks, vst, kw, vwt, gates, sz, cover_t, onehot)


def _gate_layout(v):
    per_group = NSA_GROUP * 3
    v = v.reshape(v.shape[:-1] + (NSA_KV_HEADS, per_group))
    v = jnp.pad(v, [(0, 0)] * (v.ndim - 1) + [(0, GATE_LANES - per_group)])
    return v.reshape(v.shape[:-2] + (NSA_KV_HEADS * GATE_LANES,))


def kernel(x, mem, a_norm, a_w_in, a_conv_w, a_conv_b, a_w_out, kv_norm, kv_w,
           cmp_pos_k, cmp_w1_k, cmp_w2_k, cmp_pos_v, cmp_w1_v, cmp_w2_v,
           kn_cmp, kn_slc, kn_win, b_norm, b_w_in, b_gate_bias, b_q_norm, b_w_out,
           mem_norm, mem_w_kv, mem_q_norm, mem_k_norm):
    batch, seq, d = x.shape
    conv_width = d - MEM_WIDTH
    assert a_norm.shape[0] == 1 and b_norm.shape[0] == 1 and seq % TM == 0
    x2d = x.reshape(batch * seq, d)
    mem2d = mem.reshape(batch * MEM_TOKENS, d)

    mk, mv = _mem_kv(mem2d, mem_norm, mem_w_kv.astype(BF16), mem_k_norm)

    y_conv, y_mem = _a_in(x2d, a_norm[0], a_w_in[0].astype(BF16), a_conv_w[0], a_conv_b[0],
                          mk[0], mv[0], mem_q_norm[0], seq)
    x1 = _out_proj(x2d, y_conv, y_mem, a_w_out[0].astype(BF16))

    kc, vc, ks, vst, kw, vwt = _nsa_kv(x1, kv_norm, kv_w.astype(BF16), kn_slc, kn_win, batch, seq)
    k_cmp = _compress(kc, cmp_pos_k, cmp_w1_k, cmp_w2_k, kn_cmp, True, batch, seq)
    v_cmp_t = _compress(vc, cmp_pos_v, cmp_w1_v, cmp_w2_v, kn_cmp, False, batch, seq)

    wb = b_w_in[0]
    o = 0
    w_q = wb[:, o:o + NSA_WIDTH]; o += NSA_WIDTH
    w_g = wb[:, o:o + 3 * NSA_HEADS]; o += 3 * NSA_HEADS
    w_z = wb[:, o:o + NSA_WIDTH]; o += NSA_WIDTH
    w_m = wb[:, o:]
    w_main = jnp.concatenate([w_q, w_z, w_m], axis=1).astype(BF16)
    q_gain_b = jnp.broadcast_to((b_q_norm[0] * (SCALE * LOG2E))[:, None], (HEAD_DIM, HEAD_DIM))
    qt, gates, sz, y_mem2 = _b_in(x1, b_norm[0], w_main, _gate_layout(w_g).astype(BF16),
                                  _gate_layout(b_gate_bias[0])[None, :], q_gain_b,
                                  mk[1], mv[1], mem_q_norm[1], batch, seq)
    y_nsa = _nsa_attn(qt, k_cmp, v_cmp_t, ks, vst, kw, vwt, gates, sz, batch, seq)
    out = _out_proj(x1, y_nsa, y_mem2, b_w_out[0].astype(BF16))
    return out.reshape(batch, seq, d)
```

```python
import functools

import numpy as np
import jax
import jax.numpy as jnp
from jax import lax
from jax.experimental import pallas as pl
from jax.experimental.pallas import tpu as pltpu

HEAD_DIM = 128
MEM_TOKENS = 256
MEM_HEADS = 4
MEM_WIDTH = MEM_HEADS * HEAD_DIM
NSA_KV_HEADS = 4
NSA_GROUP = 3
NSA_HEADS = NSA_KV_HEADS * NSA_GROUP
NSA_WIDTH = NSA_HEADS * HEAD_DIM
KV_BRANCH_WIDTH = NSA_KV_HEADS * HEAD_DIM
CONV_SIZE = 3
CMP_BLOCK = 32
CMP_STRIDE = 16
CMP_HIDDEN = 256
SLC_BLOCK = 64
SLC_SHIFT = 6
SLC_TOPK = 16
WINDOW = 512
EPS = 1e-6
NEG = -1e30
FORCE = 1e4
SCALE = HEAD_DIM ** -0.5
LOG2E = 1.4426950408889634

V7X_VMEM_BYTES = 64 * 1024 * 1024
V7X_LANES = 128
BF16_SUBLANES = 16

TM = 1024
HALO = BF16_SUBLANES
CONV_CW = 256
TQ = 256
KC = 256
GATE_LANES = 32
CMP_PER_TILE = TQ // CMP_STRIDE
STAB_ROW = 32
STAB_MAX = 56.0

BF16 = jnp.bfloat16
F32 = jnp.float32


def _params(vmem_mb, n_grid):
    return pltpu.CompilerParams(
        dimension_semantics=("arbitrary",) * n_grid,
        vmem_limit_bytes=vmem_mb * 1024 * 1024)


def _rms_rows(v, gain):
    return v * lax.rsqrt(jnp.mean(v * v, axis=-1, keepdims=True) + EPS) * gain


def _silu(v):
    return v * jax.nn.sigmoid(v)


def _dot(a, b):
    return jnp.dot(a, b, preferred_element_type=F32)


def _dot_nt(a, b):
    return lax.dot_general(a, b, (((1,), (1,)), ((), ())), preferred_element_type=F32)


def _mem_attention(mq, mk_ref, mv_ref, q_gain):
    outs = []
    for h in range(MEM_HEADS):
        sl = slice(h * HEAD_DIM, (h + 1) * HEAD_DIM)
        q = _rms_rows(mq[:, sl], q_gain * SCALE).astype(BF16)
        s = _dot_nt(q, mk_ref[:, sl])
        e = jnp.exp(s - jnp.max(s, axis=-1, keepdims=True))
        l = jnp.sum(e, axis=-1, keepdims=True)
        outs.append(_dot(e.astype(BF16), mv_ref[:, sl]) / l)
    return outs


def _mem_kv_kernel(mem_ref, g_ref, w_ref, kg_ref, mk_ref, mv_ref):
    h = _rms_rows(mem_ref[...], g_ref[0]).astype(BF16)
    kv = _dot(h, w_ref[0])
    for hd in range(MEM_HEADS):
        sl = slice(hd * HEAD_DIM, (hd + 1) * HEAD_DIM)
        mk_ref[0, :, sl] = _rms_rows(kv[:, sl], kg_ref[0]).astype(BF16)
    mv_ref[0] = kv[:, MEM_WIDTH:].astype(BF16)


def _mem_kv(mem2d, mem_norm, w_kv, k_gain):
    n_layers, d = mem_norm.shape
    rows = mem2d.shape[0]
    tm = min(TM, rows)
    out = jax.ShapeDtypeStruct((n_layers, rows, MEM_WIDTH), BF16)
    return pl.pallas_call(
        _mem_kv_kernel,
        grid=(n_layers, rows // tm),
        in_specs=[
            pl.BlockSpec((tm, d), lambda l, i: (i, 0)),
            pl.BlockSpec((1, 1, d), lambda l, i: (l, 0, 0)),
            pl.BlockSpec((1, d, 2 * MEM_WIDTH), lambda l, i: (l, 0, 0)),
            pl.BlockSpec((1, 1, HEAD_DIM), lambda l, i: (l, 0, 0)),
        ],
        out_specs=[pl.BlockSpec((1, tm, MEM_WIDTH), lambda l, i: (l, i, 0))] * 2,
        out_shape=[out, out],
        compiler_params=_params(40, 2),
        name="mem_kv",
    )(mem2d, mem_norm[:, None, :], w_kv, k_gain[:, None, :])


def _a_in_kernel(x_ref, xh_ref, g_ref, wb_ref, wc_ref, wh_ref, wz_ref, wm_ref, cw_ref, cb_ref,
                 mk_ref, mv_ref, qg_ref, yc_ref, ym_ref, h_ref, *, tiles_per_seq, n_conv):
    i = pl.program_id(0)
    c = pl.program_id(1)
    tm = x_ref.shape[0]

    @pl.when(c == 0)
    def _():
        keep = (i % tiles_per_seq != 0).astype(F32)
        h_ref[0:HALO, :] = (_rms_rows(xh_ref[...], g_ref[...]) * keep).astype(BF16)
        h_ref[HALO:, :] = _rms_rows(x_ref[...], g_ref[...]).astype(BF16)

    @pl.when(c < n_conv)
    def _():
        u = _dot(h_ref[...], wc_ref[...]) * _dot(h_ref[...], wh_ref[...])
        taps = cw_ref[...]
        y = (taps[0:1] * pltpu.roll(u, 2, axis=0)[HALO:]
             + taps[1:2] * pltpu.roll(u, 1, axis=0)[HALO:]
             + taps[2:3] * u[HALO:])
        h = h_ref[HALO:, :]
        yc_ref[...] = (_dot(h, wb_ref[...]) * (y + cb_ref[...])
                       * _silu(_dot(h, wz_ref[...]))).astype(BF16)

    @pl.when(c == n_conv)
    def _():
        acc = _dot(h_ref[HALO:, :], wm_ref[...])
        outs = _mem_attention(acc[:, :MEM_WIDTH], mk_ref, mv_ref, qg_ref[...])
        for hd in range(MEM_HEADS):
            sl = slice(hd * HEAD_DIM, (hd + 1) * HEAD_DIM)
            ym_ref[:, sl] = (outs[hd] * _silu(acc[:, MEM_WIDTH + hd * HEAD_DIM:
                                                  MEM_WIDTH + (hd + 1) * HEAD_DIM])).astype(BF16)


def _a_in(x2d, norm, w, conv_w, conv_b, mk, mv, q_gain, seq):
    t, d = x2d.shape
    conv_width = conv_w.shape[1]
    n_conv = conv_width // CONV_CW
    mem_cols = 2 * MEM_WIDTH
    assert w.shape[1] == 4 * conv_width + mem_cols and (4 * conv_width) % mem_cols == 0
    tiles_per_seq = seq // TM
    halo_blocks = TM // HALO
    kern = functools.partial(_a_in_kernel, tiles_per_seq=tiles_per_seq, n_conv=n_conv)
    last = n_conv - 1

    def conv_cols(part):
        return pl.BlockSpec((d, CONV_CW), lambda i, c: (0, part * n_conv + jnp.minimum(c, last)))

    return pl.pallas_call(
        kern,
        grid=(t // TM, n_conv + 1),
        in_specs=[
            pl.BlockSpec((TM, d), lambda i, c: (i, 0)),
            pl.BlockSpec((HALO, d), lambda i, c: (jnp.maximum(i * halo_blocks - 1, 0), 0)),
            pl.BlockSpec((1, d), lambda i, c: (0, 0)),
            conv_cols(0), conv_cols(1), conv_cols(2), conv_cols(3),
            pl.BlockSpec((d, mem_cols), lambda i, c: (0, 4 * conv_width // mem_cols)),
            pl.BlockSpec((CONV_SIZE, CONV_CW), lambda i, c: (0, jnp.minimum(c, last))),
            pl.BlockSpec((1, CONV_CW), lambda i, c: (0, jnp.minimum(c, last))),
            pl.BlockSpec((MEM_TOKENS, MEM_WIDTH), lambda i, c: (i // tiles_per_seq, 0)),
            pl.BlockSpec((MEM_TOKENS, MEM_WIDTH), lambda i, c: (i // tiles_per_seq, 0)),
            pl.BlockSpec((1, HEAD_DIM), lambda i, c: (0, 0)),
        ],
        out_specs=[
            pl.BlockSpec((TM, CONV_CW), lambda i, c: (i, jnp.minimum(c, last))),
            pl.BlockSpec((TM, MEM_WIDTH), lambda i, c: (i, 0)),
        ],
        out_shape=[jax.ShapeDtypeStruct((t, conv_width), BF16),
                   jax.ShapeDtypeStruct((t, MEM_WIDTH), BF16)],
        scratch_shapes=[pltpu.VMEM((TM + HALO, d), BF16)],
        compiler_params=_params(56, 2),
        name="a_in",
    )(x2d, x2d, norm[None, :], w, w, w, w, w, conv_w, conv_b[None, :], mk, mv, q_gain[None, :])


def _out_proj_kernel(res_ref, y1_ref, y2_ref, w1_ref, w2_ref, o_ref):
    o_ref[...] = res_ref[...] + _dot(y1_ref[...], w1_ref[...]) + _dot(y2_ref[...], w2_ref[...])


def _out_proj(res, y1, y2, w):
    t, d = res.shape
    tn = 1024
    n1, n2 = y1.shape[1], y2.shape[1]
    assert n1 % n2 == 0 and n1 + n2 == w.shape[0]
    return pl.pallas_call(
        _out_proj_kernel,
        grid=(t // TM, d // tn),
        in_specs=[
            pl.BlockSpec((TM, tn), lambda i, j: (i, j)),
            pl.BlockSpec((TM, n1), lambda i, j: (i, 0)),
            pl.BlockSpec((TM, n2), lambda i, j: (i, 0)),
            pl.BlockSpec((n1, tn), lambda i, j: (0, j)),
            pl.BlockSpec((n2, tn), lambda i, j: (n1 // n2, j)),
        ],
        out_specs=pl.BlockSpec((TM, tn), lambda i, j: (i, j)),
        out_shape=jax.ShapeDtypeStruct((t, d), F32),
        compiler_params=_params(48, 2),
        name="out_proj",
    )(res, y1, y2, w, w)


def _store_transposed(dst_ref, val):
    vt = val.T.astype(BF16)
    for g in range(NSA_KV_HEADS):
        for cc in range(val.shape[0] // KC):
            dst_ref[0, g, cc] = vt[g * HEAD_DIM:(g + 1) * HEAD_DIM, cc * KC:(cc + 1) * KC]


def _nsa_kv_kernel(x_ref, g_ref, w_ref, kns_ref, knw_ref,
                   kc_ref, vc_ref, ks_ref, vst_ref, kw_ref, vwt_ref, h_ref):
    j = pl.program_id(1)

    @pl.when(j == 0)
    def _():
        h_ref[...] = _rms_rows(x_ref[...], g_ref[...]).astype(BF16)

    def normed_keys(acc, gain):
        return jnp.concatenate(
            [_rms_rows(acc[:, g * HEAD_DIM:(g + 1) * HEAD_DIM], gain).astype(BF16)
             for g in range(NSA_KV_HEADS)], axis=1)

    @pl.when(j == 0)
    def _():
        kc_ref[...] = _dot(h_ref[...], w_ref[...])

    @pl.when(j == 1)
    def _():
        vc_ref[...] = _dot(h_ref[...], w_ref[...])

    @pl.when(j == 2)
    def _():
        ks_ref[...] = normed_keys(_dot(h_ref[...], w_ref[...]), kns_ref[...])

    @pl.when(j == 3)
    def _():
        _store_transposed(vst_ref, _dot(h_ref[...], w_ref[...]))

    @pl.when(j == 4)
    def _():
        kw_ref[...] = normed_keys(_dot(h_ref[...], w_ref[...]), knw_ref[...])

    @pl.when(j == 5)
    def _():
        _store_transposed(vwt_ref, _dot(h_ref[...], w_ref[...]))


def _nsa_kv(x2d, norm, w, kn_slc, kn_win, batch, seq):
    t, d = x2d.shape
    bw = KV_BRANCH_WIDTH
    tiles_per_seq = seq // TM
    row = lambda i, j: (i, 0)
    vt_spec = pl.BlockSpec((1, NSA_KV_HEADS, TM // KC, HEAD_DIM, KC),
                           lambda i, j: (i // tiles_per_seq, 0, i % tiles_per_seq, 0, 0))
    vt_shape = jax.ShapeDtypeStruct((batch, NSA_KV_HEADS, seq // KC, HEAD_DIM, KC), BF16)
    return pl.pallas_call(
        _nsa_kv_kernel,
        grid=(t // TM, 6),
        in_specs=[
            pl.BlockSpec((TM, d), row),
            pl.BlockSpec((1, d), lambda i, j: (0, 0)),
            pl.BlockSpec((d, bw), lambda i, j: (0, j)),
            pl.BlockSpec((1, HEAD_DIM), lambda i, j: (0, 0)),
            pl.BlockSpec((1, HEAD_DIM), lambda i, j: (0, 0)),
        ],
        out_specs=[pl.BlockSpec((TM, bw), row), pl.BlockSpec((TM, bw), row),
                   pl.BlockSpec((TM, bw), row), vt_spec,
                   pl.BlockSpec((TM, bw), row), vt_spec],
        out_shape=[jax.ShapeDtypeStruct((t, bw), F32), jax.ShapeDtypeStruct((t, bw), F32),
                   jax.ShapeDtypeStruct((t, bw), BF16), vt_shape,
                   jax.ShapeDtypeStruct((t, bw), BF16), vt_shape],
        scratch_shapes=[pltpu.VMEM((TM, d), BF16)],
        compiler_params=_params(48, 2),
        name="nsa_kv",
    )(x2d, norm[None, :], w, kn_slc[None, :], kn_win[None, :])


def _compress_kernel(*refs, is_key):
    x_refs = refs[:NSA_KV_HEADS]
    pos_ref, w1_ref, w2_ref, kn_ref, o_ref = refs[NSA_KV_HEADS:]
    n_chunks = x_refs[0].shape[0] // CMP_STRIDE
    pairs = CMP_STRIDE // 2
    first = second = None
    for p in range(pairs):
        lhs_a, lhs_b = [], []
        for l in (2 * p, 2 * p + 1):
            xg = jnp.concatenate([x_ref[pl.ds(l, n_chunks, stride=CMP_STRIDE), :]
                                  for x_ref in x_refs], axis=0)
            lhs_a.append((xg + pos_ref[l:l + 1, :]).astype(BF16))
            lhs_b.append((xg + pos_ref[CMP_STRIDE + l:CMP_STRIDE + l + 1, :]).astype(BF16))
        da = _dot(jnp.concatenate(lhs_a, axis=1), w1_ref[p])
        db = _dot(jnp.concatenate(lhs_b, axis=1), w1_ref[pairs + p])
        first = da if first is None else first + da
        second = db if second is None else second + db
    rows = first.shape[0]
    hid = _silu(first + pltpu.roll(second, rows - 1, axis=0))
    o = _dot(hid.astype(BF16), w2_ref[...])
    if is_key:
        o = _rms_rows(o, kn_ref[...])
    chunk = lax.broadcasted_iota(jnp.int32, o.shape, 0) & (n_chunks - 1)
    o = jnp.where(chunk == n_chunks - 1, 0.0, o)
    if is_key:
        o_ref[...] = o.astype(BF16)
    else:
        for g in range(NSA_KV_HEADS):
            o_ref[g] = o[g * n_chunks:(g + 1) * n_chunks, :].T.astype(BF16)


def _compress(t2d, pos, w1, w2, kn, is_key, batch, seq):
    assert t2d.shape[1] == NSA_KV_HEADS * HEAD_DIM
    n_chunks = seq // CMP_STRIDE
    rows = NSA_KV_HEADS * n_chunks
    w1p = w1.reshape(CMP_BLOCK // 2, 2 * HEAD_DIM, CMP_HIDDEN).astype(BF16)
    kern = functools.partial(_compress_kernel, is_key=is_key)
    if is_key:
        out_spec = pl.BlockSpec((rows, HEAD_DIM), lambda b: (b, 0))
        out_shape = jax.ShapeDtypeStruct((batch * rows, HEAD_DIM), BF16)
    else:
        out_spec = pl.BlockSpec((NSA_KV_HEADS, HEAD_DIM, n_chunks), lambda b: (b, 0, 0))
        out_shape = jax.ShapeDtypeStruct((batch * NSA_KV_HEADS, HEAD_DIM, n_chunks), BF16)
    return pl.pallas_call(
        kern,
        grid=(batch,),
        in_specs=[pl.BlockSpec((seq, HEAD_DIM), functools.partial(lambda b, g: (b, g), g=g))
                  for g in range(NSA_KV_HEADS)] + [
            pl.BlockSpec((CMP_BLOCK, HEAD_DIM), lambda b: (0, 0)),
            pl.BlockSpec((CMP_BLOCK // 2, 2 * HEAD_DIM, CMP_HIDDEN), lambda b: (0, 0, 0)),
            pl.BlockSpec((CMP_HIDDEN, HEAD_DIM), lambda b: (0, 0)),
            pl.BlockSpec((1, HEAD_DIM), lambda b: (0, 0)),
        ],
        out_specs=out_spec,
        out_shape=out_shape,
        compiler_params=_params(32, 1),
        name="compress_k" if is_key else "compress_v",
    )(*([t2d] * NSA_KV_HEADS), pos, w1p, w2.astype(BF16), kn[None, :])


def _b_in_kernel(x_ref, g_ref, w_ref, wg_ref, gb_ref, qg_ref, mk_ref, mv_ref, mqg_ref,
                 qt_ref, gate_ref, sz_ref, ym_ref, h_ref, om_ref):
    j = pl.program_id(1)
    tm = x_ref.shape[0]
    heads_per_step = w_ref.shape[1] // HEAD_DIM

    @pl.when(j == 0)
    def _():
        h_ref[...] = _rms_rows(x_ref[...], g_ref[...]).astype(BF16)
        gate_ref[...] = jax.nn.sigmoid(_dot(h_ref[...], wg_ref[...]) + gb_ref[...])

    @pl.when(j < 3)
    def _():
        acc = _dot(h_ref[...], w_ref[...])
        gain = jnp.tile(qg_ref[...], (1, tm // HEAD_DIM))
        for hd in range(heads_per_step):
            qt = acc[:, hd * HEAD_DIM:(hd + 1) * HEAD_DIM].T
            inv = lax.rsqrt(jnp.mean(qt * qt, axis=0, keepdims=True) + EPS)
            qt_ref[0, hd * HEAD_DIM:(hd + 1) * HEAD_DIM, :] = (qt * inv * gain).astype(BF16)

    @pl.when((j >= 3) & (j < 6))
    def _():
        sz_ref[...] = _silu(_dot(h_ref[...], w_ref[...])).astype(BF16)

    @pl.when(j == 6)
    def _():
        outs = _mem_attention(_dot(h_ref[...], w_ref[...]), mk_ref, mv_ref, mqg_ref[...])
        for hd in range(MEM_HEADS):
            om_ref[:, hd * HEAD_DIM:(hd + 1) * HEAD_DIM] = outs[hd]

    @pl.when(j == 7)
    def _():
        ym_ref[...] = (om_ref[...] * _silu(_dot(h_ref[...], w_ref[...]))).astype(BF16)


def _b_in(x2d, norm, w_main, w_gate, gate_bias, q_gain_b, mk, mv, mem_q_gain, batch, seq):
    t, d = x2d.shape
    tn = 512
    tiles_per_seq = seq // TM
    const = lambda i, j: (0, 0)
    mem_idx = lambda i, j: (i // tiles_per_seq, 0)
    return pl.pallas_call(
        _b_in_kernel,
        grid=(t // TM, 8),
        in_specs=[
            pl.BlockSpec((TM, d), lambda i, j: (i, 0)),
            pl.BlockSpec((1, d), const),
            pl.BlockSpec((d, tn), lambda i, j: (0, j)),
            pl.BlockSpec((d, V7X_LANES), const),
            pl.BlockSpec((1, V7X_LANES), const),
            pl.BlockSpec((HEAD_DIM, HEAD_DIM), const),
            pl.BlockSpec((MEM_TOKENS, MEM_WIDTH), mem_idx),
            pl.BlockSpec((MEM_TOKENS, MEM_WIDTH), mem_idx),
            pl.BlockSpec((1, HEAD_DIM), const),
        ],
        out_specs=[
            pl.BlockSpec((1, tn, TM), lambda i, j: (i // tiles_per_seq, jnp.minimum(j, 2),
                                                     i % tiles_per_seq)),
            pl.BlockSpec((TM, V7X_LANES), lambda i, j: (i, 0)),
            pl.BlockSpec((TM, tn), lambda i, j: (i, jnp.clip(j - 3, 0, 2))),
            pl.BlockSpec((TM, MEM_WIDTH), lambda i, j: (i, 0)),
        ],
        out_shape=[jax.ShapeDtypeStruct((batch, NSA_WIDTH, seq), BF16),
                   jax.ShapeDtypeStruct((t, V7X_LANES), F32),
                   jax.ShapeDtypeStruct((t, NSA_WIDTH), BF16),
                   jax.ShapeDtypeStruct((t, MEM_WIDTH), BF16)],
        scratch_shapes=[pltpu.VMEM((TM, d), BF16), pltpu.VMEM((TM, MEM_WIDTH), F32)],
        compiler_params=_params(48, 2),
        name="b_in",
    )(x2d, norm[None, :], w_main, w_gate, gate_bias, q_gain_b, mk, mv, mem_q_gain[None, :])


def _attn_block(qi, fast, q3, gates, stab, kcmp_ref, vcmpt_ref, ks_ref, vst_ref, kw_ref, vwt_ref,
                sz_ref, cover_ref, onehot_ref, y_ref, ssel_ref, swin_ref):
    n = NSA_GROUP * TQ
    qs = qi * TQ
    q = [q3[j * HEAD_DIM:(j + 1) * HEAD_DIM] for j in range(NSA_GROUP)]
    k_loc = lax.broadcasted_iota(jnp.int32, (KC, TQ), 0)
    t_loc = lax.broadcasted_iota(jnp.int32, (KC, TQ), 1)
    causal = k_loc <= t_loc
    band = k_loc > t_loc
    sel_chunks = range(qi + 1)
    win_chunks = range(max(qi - WINDOW // KC, 0), qi + 1)
    n_allowed = (qs + TQ - 1) // SLC_BLOCK + 1
    use_bias = n_allowed > SLC_TOPK

    ones_rows = jnp.ones((BF16_SUBLANES, KC), BF16)

    def scores_pass(is_sel, j, sel_bias=None):
        k_ref, s_ref = (ks_ref, ssel_ref) if is_sel else (kw_ref, swin_ref)
        qj = q[j]
        augmented = fast or sel_bias is not None
        if augmented:
            extra = jnp.zeros((HEAD_DIM, TQ), F32) if sel_bias is None else jnp.concatenate(
                [sel_bias, jnp.zeros((HEAD_DIM - sel_bias.shape[0], TQ), F32)], axis=0)
            if fast:
                row = lax.broadcasted_iota(jnp.int32, (HEAD_DIM, TQ), 0)
                extra = jnp.where(row == STAB_ROW, -stab, extra)
            qj = jnp.concatenate([qj, extra.astype(BF16)], axis=0)
        m = None
        for i, ck in enumerate(sel_chunks if is_sel else win_chunks):
            k = k_ref[ck * KC:(ck + 1) * KC, :]
            if augmented:
                k = jnp.concatenate([k, onehot_ref[ck * KC:(ck + 1) * KC, :]], axis=1)
            sc = _dot(k, qj)
            if not is_sel and ck == qi - WINDOW // KC:
                sc = jnp.where(band, sc, NEG)
            if ck == qi:
                sc = jnp.where(causal, sc, NEG)
            s_ref[j, i * KC:(i + 1) * KC, :] = sc
            if not fast:
                m_c = jnp.max(sc, axis=0, keepdims=True)
                m = m_c if m is None else jnp.maximum(m, m_c)
        return m

    def values_pass(is_sel, j, m):
        vt_ref, s_ref = (vst_ref, ssel_ref) if is_sel else (vwt_ref, swin_ref)
        l = acc = None
        for i, ck in enumerate(sel_chunks if is_sel else win_chunks):
            sc = s_ref[j, i * KC:(i + 1) * KC, :]
            vt = vt_ref[0, 0, ck]
            if fast:
                pe = jnp.exp2(sc)
                vt = jnp.concatenate([vt, ones_rows], axis=0)
            else:
                pe = jnp.exp2(sc - m)
                l_c = jnp.sum(pe, axis=0, keepdims=True)
                l = l_c if l is None else l + l_c
            pv = _dot(vt, pe.astype(BF16))
            acc = pv if acc is None else acc + pv
        if fast:
            acc, l = acc[:HEAD_DIM], acc[HEAD_DIM:HEAD_DIM + 1]
        return acc, l

    def merge(j, o_cmp_j, sel_j, win_j):
        yt = (gates[3 * j:3 * j + 1] * o_cmp_j
              + (gates[3 * j + 1:3 * j + 2] / sel_j[1]) * sel_j[0]
              + (gates[3 * j + 2:3 * j + 3] / win_j[1]) * win_j[0])
        ch = slice(j * HEAD_DIM, (j + 1) * HEAD_DIM)
        y_ref[:, ch] = (yt.T * sz_ref[:, ch].astype(F32)).astype(BF16)

    n_cmp = kcmp_ref.shape[0]
    n_vis = min(n_cmp, CMP_PER_TILE * (qi + 1))
    s_cmp = _dot(kcmp_ref[0:n_vis, :], jnp.concatenate(q, axis=1))
    m_win = [scores_pass(False, 0)]
    top = max(n_vis - 2 * CMP_PER_TILE, 0)
    c_idx = top + lax.broadcasted_iota(jnp.int32, (n_vis - top, n), 0)
    t_idx = qs + (lax.broadcasted_iota(jnp.int32, (n_vis - top, n), 1) & (TQ - 1))
    slab_ok = c_idx * CMP_STRIDE + (CMP_BLOCK - 1) <= t_idx

    def with_slab(v, fill):
        slab = jnp.where(slab_ok, v[top:], fill)
        return slab if top == 0 else jnp.concatenate([v[:top], slab], axis=0)

    sm = with_slab(s_cmp, NEG)
    e = jnp.exp2(sm - jnp.max(sm, axis=0, keepdims=True))
    p = with_slab(e / jnp.sum(e, axis=0, keepdims=True), 0.0)
    if n_vis < n_cmp:
        p = jnp.concatenate([p, jnp.zeros((n_cmp - n_vis, n), F32)], axis=0)
    o_cmp = _dot(vcmpt_ref[0], p.astype(BF16))
    m_win.append(scores_pass(False, 1))

    sel_bias = None
    if use_bias:
        imp = p[:, 0:TQ] + p[:, TQ:2 * TQ] + p[:, 2 * TQ:3 * TQ]
        hi = imp.astype(BF16)
        r1 = imp - hi.astype(F32)
        mid = r1.astype(BF16)
        lo = (r1 - mid.astype(F32)).astype(BF16)
        cover = cover_ref[...]
        score = _dot(cover, hi) + _dot(cover, mid) + _dot(cover, lo)
        n_sel = score.shape[0]
        j_idx = lax.broadcasted_iota(jnp.int32, (n_sel, TQ), 0)
        tq_idx = qs + lax.broadcasted_iota(jnp.int32, (n_sel, TQ), 1)
        cur = tq_idx >> SLC_SHIFT
        allowed = j_idx * SLC_BLOCK <= tq_idx
        forced = (j_idx == 0) | (j_idx == cur) | (j_idx == cur - 1)
        score = jnp.where(forced, FORCE, jnp.where(allowed, score, NEG))
        rank = jnp.zeros((n_sel, TQ), jnp.int32)
        for jp in range(n_allowed):
            row = score[jp:jp + 1, :]
            before = (row > score) | ((row == score) & (j_idx > jp))
            rank = rank + before.astype(jnp.int32)
        sel_bias = jnp.where(allowed & (rank < SLC_TOPK), 0.0, NEG)

    win = [values_pass(False, 0, m_win[0])]
    m_win.append(scores_pass(False, 2))
    win.append(values_pass(False, 1, m_win[1]))
    m_sel = [scores_pass(True, 0, sel_bias)]
    win.append(values_pass(False, 2, m_win[2]))
    for j in range(NSA_GROUP):
        if j + 1 < NSA_GROUP:
            m_sel.append(scores_pass(True, j + 1, sel_bias))
        merge(j, o_cmp[:, j * TQ:(j + 1) * TQ], values_pass(True, j, m_sel[j]), win[j])


def _nsa_attn_kernel(stab_ref, qt_ref, kcmp_ref, vcmpt_ref, ks_ref, vst_ref, kw_ref, vwt_ref,
                     gate_ref, sz_ref, cover_ref, onehot_ref, y_ref, gt_ref, ssel_ref, swin_ref):
    g = pl.program_id(1)
    qi = pl.program_id(2)

    def tile_body(q, fast):
        gt_ref[...] = gate_ref[...].T
        gates = gt_ref[pl.ds(pl.multiple_of(g * GATE_LANES, GATE_LANES), GATE_LANES), :]
        _attn_block(q, fast, qt_ref[0], gates, stab, kcmp_ref, vcmpt_ref, ks_ref, vst_ref,
                    kw_ref, vwt_ref, sz_ref, cover_ref, onehot_ref, y_ref, ssel_ref, swin_ref)

    stab = stab_ref[0]
    for q in range(ks_ref.shape[0] // TQ):
        for fast in (True, False):
            pl.when((qi == q) & ((stab <= STAB_MAX) == fast))(
                functools.partial(tile_body, q, fast))


def _nsa_attn(stab, qt, kcmp, vcmpt, ks, vst, kw, vwt, gates, sz, batch, seq):
    n_q = seq // TQ
    n_cmp = seq // CMP_STRIDE
    n_sel = seq // SLC_BLOCK
    n = NSA_GROUP * TQ
    i = np.arange(n_cmp)[:, None]
    j = np.arange(n_sel)[None, :]
    cover = ((i * CMP_STRIDE < (j + 1) * SLC_BLOCK) & (i * CMP_STRIDE + CMP_BLOCK > j * SLC_BLOCK))
    cover_t = jnp.asarray(cover.T, BF16)
    lane = np.arange(HEAD_DIM)[None, :]
    onehot = jnp.asarray((np.arange(seq)[:, None] // SLC_BLOCK == lane) | (lane == STAB_ROW), BF16)
    kv_spec = pl.BlockSpec((seq, HEAD_DIM), lambda b, g, q: (b, g))
    vt_spec = pl.BlockSpec((1, 1, seq // KC, HEAD_DIM, KC), lambda b, g, q: (b, g, 0, 0, 0))
    gw = NSA_GROUP * HEAD_DIM
    return pl.pallas_call(
        _nsa_attn_kernel,
        grid=(batch, NSA_KV_HEADS, n_q),
        in_specs=[
            pl.BlockSpec(memory_space=pltpu.SMEM),
            pl.BlockSpec((1, gw, TQ), lambda b, g, q: (b, g, q)),
            pl.BlockSpec((n_cmp, HEAD_DIM), lambda b, g, q: (b * NSA_KV_HEADS + g, 0)),
            pl.BlockSpec((1, HEAD_DIM, n_cmp), lambda b, g, q: (b * NSA_KV_HEADS + g, 0, 0)),
            kv_spec, vt_spec, kv_spec, vt_spec,
            pl.BlockSpec((TQ, V7X_LANES), lambda b, g, q: (b * n_q + q, 0)),
            pl.BlockSpec((TQ, gw), lambda b, g, q: (b * n_q + q, g)),
            pl.BlockSpec((n_sel, n_cmp), lambda b, g, q: (0, 0)),
            pl.BlockSpec((seq, HEAD_DIM), lambda b, g, q: (0, 0)),
        ],
        out_specs=pl.BlockSpec((TQ, gw), lambda b, g, q: (b * n_q + q, g)),
        out_shape=jax.ShapeDtypeStruct((batch * seq, NSA_WIDTH), BF16),
        scratch_shapes=[pltpu.VMEM((V7X_LANES, TQ), F32),
                        pltpu.VMEM((NSA_GROUP, seq, TQ), F32),
                        pltpu.VMEM((NSA_GROUP, WINDOW + KC, TQ), F32)],
        compiler_params=_params(40, 3),
        name="nsa_attn",
    )(stab, qt, kcmp, vcmpt, ks, vst, kw, vwt, gates, sz, cover_t, onehot)


def _gate_layout(v):
    per_group = NSA_GROUP * 3
    v = v.reshape(v.shape[:-1] + (NSA_KV_HEADS, per_group))
    v = jnp.pad(v, [(0, 0)] * (v.ndim - 1) + [(0, GATE_LANES - per_group)])
    return v.reshape(v.shape[:-2] + (NSA_KV_HEADS * GATE_LANES,))


def kernel(x, mem, a_norm, a_w_in, a_conv_w, a_conv_b, a_w_out, kv_norm, kv_w,
           cmp_pos_k, cmp_w1_k, cmp_w2_k, cmp_pos_v, cmp_w1_v, cmp_w2_v,
           kn_cmp, kn_slc, kn_win, b_norm, b_w_in, b_gate_bias, b_q_norm, b_w_out,
           mem_norm, mem_w_kv, mem_q_norm, mem_k_norm):
    batch, seq, d = x.shape
    conv_width = d - MEM_WIDTH
    assert a_norm.shape[0] == 1 and b_norm.shape[0] == 1 and seq % TM == 0
    x2d = x.reshape(batch * seq, d)
    mem2d = mem.reshape(batch * MEM_TOKENS, d)

    mk, mv = _mem_kv(mem2d, mem_norm, mem_w_kv.astype(BF16), mem_k_norm)

    y_conv, y_mem = _a_in(x2d, a_norm[0], a_w_in[0].astype(BF16), a_conv_w[0], a_conv_b[0],
                          mk[0], mv[0], mem_q_norm[0], seq)
    x1 = _out_proj(x2d, y_conv, y_mem, a_w_out[0].astype(BF16))

    kc, vc, ks, vst, kw, vwt = _nsa_kv(x1, kv_norm, kv_w.astype(BF16), kn_slc, kn_win, batch, seq)
    k_cmp = _compress(kc, cmp_pos_k, cmp_w1_k, cmp_w2_k, kn_cmp, True, batch, seq)
    v_cmp_t = _compress(vc, cmp_pos_v, cmp_w1_v, cmp_w2_v, kn_cmp, False, batch, seq)

    wb = b_w_in[0]
    o = 0
    w_q = wb[:, o:o + NSA_WIDTH]; o += NSA_WIDTH
    w_g = wb[:, o:o + 3 * NSA_HEADS]; o += 3 * NSA_HEADS
    w_z = wb[:, o:o + NSA_WIDTH]; o += NSA_WIDTH
    w_m = wb[:, o:]
    w_main = jnp.concatenate([w_q, w_z, w_m], axis=1).astype(BF16)
    q_gain_b = jnp.broadcast_to((b_q_norm[0] * (SCALE * LOG2E))[:, None], (HEAD_DIM, HEAD_DIM))
    qt, gates, sz, y_mem2 = _b_in(x1, b_norm[0], w_main, _gate_layout(w_g).astype(BF16),
                                  _gate_layout(b_gate_bias[0])[None, :], q_gain_b,
                                  mk[1], mv[1], mem_q_norm[1], batch, seq)
    stab = (1.02 * HEAD_DIM * SCALE * LOG2E * jnp.max(jnp.abs(b_q_norm[0]))
            * jnp.maximum(jnp.max(jnp.abs(kn_slc)), jnp.max(jnp.abs(kn_win)))).reshape(1)
    y_nsa = _nsa_attn(stab, qt, k_cmp, v_cmp_t, ks, vst, kw, vwt, gates, sz, batch, seq)
    out = _out_proj(x1, y_nsa, y_mem2, b_w_out[0].astype(BF16))
    return out.reshape(batch, seq, d)
```

```python
import functools

import numpy as np
import jax
import jax.numpy as jnp
from jax import lax
from jax.experimental import pallas as pl
from jax.experimental.pallas import tpu as pltpu

HEAD_DIM = 128
MEM_TOKENS = 256
MEM_HEADS = 4
MEM_WIDTH = MEM_HEADS * HEAD_DIM
NSA_KV_HEADS = 4
NSA_GROUP = 3
NSA_HEADS = NSA_KV_HEADS * NSA_GROUP
NSA_WIDTH = NSA_HEADS * HEAD_DIM
KV_BRANCH_WIDTH = NSA_KV_HEADS * HEAD_DIM
CONV_SIZE = 3
CMP_BLOCK = 32
CMP_STRIDE = 16
CMP_HIDDEN = 256
SLC_BLOCK = 64
SLC_SHIFT = 6
SLC_TOPK = 16
WINDOW = 512
EPS = 1e-6
NEG = -1e30
FORCE = 1e4
SCALE = HEAD_DIM ** -0.5
LOG2E = 1.4426950408889634

V7X_VMEM_BYTES = 64 * 1024 * 1024
V7X_LANES = 128
BF16_SUBLANES = 16

TM = 1024
TM_RESIDENT = 512
HALO = BF16_SUBLANES
CONV_CW = 256
TQ = 256
TILES_PER_STEP = 4
KC = 256
GATE_LANES = 32
CMP_PER_TILE = TQ // CMP_STRIDE
STAB_ROW = 32
STAB_MAX = 56.0

BF16 = jnp.bfloat16
F32 = jnp.float32


def _params(vmem_mb, n_grid):
    return pltpu.CompilerParams(
        dimension_semantics=("arbitrary",) * n_grid,
        vmem_limit_bytes=vmem_mb * 1024 * 1024)


def _rms_rows(v, gain):
    return v * lax.rsqrt(jnp.mean(v * v, axis=-1, keepdims=True) + EPS) * gain


def _silu(v):
    return v * jax.nn.sigmoid(v)


def _dot(a, b):
    return jnp.dot(a, b, preferred_element_type=F32)


def _dot_nt(a, b):
    return lax.dot_general(a, b, (((1,), (1,)), ((), ())), preferred_element_type=F32)


def _mem_attention(mq, mk_ref, mv_ref, q_gain):
    outs = []
    for h in range(MEM_HEADS):
        sl = slice(h * HEAD_DIM, (h + 1) * HEAD_DIM)
        q = _rms_rows(mq[:, sl], q_gain * SCALE).astype(BF16)
        s = _dot_nt(q, mk_ref[:, sl])
        e = jnp.exp(s - jnp.max(s, axis=-1, keepdims=True))
        l = jnp.sum(e, axis=-1, keepdims=True)
        outs.append(_dot(e.astype(BF16), mv_ref[:, sl]) / l)
    return outs


def _mem_kv_kernel(mem_ref, g_ref, w_ref, kg_ref, mk_ref, mv_ref):
    h = _rms_rows(mem_ref[...], g_ref[0]).astype(BF16)
    kv = _dot(h, w_ref[0])
    for hd in range(MEM_HEADS):
        sl = slice(hd * HEAD_DIM, (hd + 1) * HEAD_DIM)
        mk_ref[0, :, sl] = _rms_rows(kv[:, sl], kg_ref[0]).astype(BF16)
    mv_ref[0] = kv[:, MEM_WIDTH:].astype(BF16)


def _mem_kv(mem2d, mem_norm, w_kv, k_gain):
    n_layers, d = mem_norm.shape
    rows = mem2d.shape[0]
    tm = min(TM, rows)
    out = jax.ShapeDtypeStruct((n_layers, rows, MEM_WIDTH), BF16)
    return pl.pallas_call(
        _mem_kv_kernel,
        grid=(n_layers, rows // tm),
        in_specs=[
            pl.BlockSpec((tm, d), lambda l, i: (i, 0)),
            pl.BlockSpec((1, 1, d), lambda l, i: (l, 0, 0)),
            pl.BlockSpec((1, d, 2 * MEM_WIDTH), lambda l, i: (l, 0, 0)),
            pl.BlockSpec((1, 1, HEAD_DIM), lambda l, i: (l, 0, 0)),
        ],
        out_specs=[pl.BlockSpec((1, tm, MEM_WIDTH), lambda l, i: (l, i, 0))] * 2,
        out_shape=[out, out],
        compiler_params=_params(40, 2),
        name="mem_kv",
    )(mem2d, mem_norm[:, None, :], w_kv, k_gain[:, None, :])


def _a_in_kernel(x_ref, xh_ref, g_ref, wb_ref, wc_ref, wh_ref, wz_ref, wm_ref, cw_ref, cb_ref,
                 mk_ref, mv_ref, qg_ref, yc_ref, ym_ref, h_ref, *, tiles_per_seq, n_conv):
    i = pl.program_id(0)
    c = pl.program_id(1)
    tm = x_ref.shape[0]

    @pl.when(c == 0)
    def _():
        keep = (i % tiles_per_seq != 0).astype(F32)
        h_ref[0:HALO, :] = (_rms_rows(xh_ref[...], g_ref[...]) * keep).astype(BF16)
        h_ref[HALO:, :] = _rms_rows(x_ref[...], g_ref[...]).astype(BF16)

    @pl.when(c < n_conv)
    def _():
        u = _dot(h_ref[...], wc_ref[...]) * _dot(h_ref[...], wh_ref[...])
        taps = cw_ref[...]
        y = (taps[0:1] * pltpu.roll(u, 2, axis=0)[HALO:]
             + taps[1:2] * pltpu.roll(u, 1, axis=0)[HALO:]
             + taps[2:3] * u[HALO:])
        h = h_ref[HALO:, :]
        yc_ref[...] = (_dot(h, wb_ref[...]) * (y + cb_ref[...])
                       * _silu(_dot(h, wz_ref[...]))).astype(BF16)

    @pl.when(c == n_conv)
    def _():
        acc = _dot(h_ref[HALO:, :], wm_ref[...])
        outs = _mem_attention(acc[:, :MEM_WIDTH], mk_ref, mv_ref, qg_ref[...])
        for hd in range(MEM_HEADS):
            sl = slice(hd * HEAD_DIM, (hd + 1) * HEAD_DIM)
            ym_ref[:, sl] = (outs[hd] * _silu(acc[:, MEM_WIDTH + hd * HEAD_DIM:
                                                  MEM_WIDTH + (hd + 1) * HEAD_DIM])).astype(BF16)


def _a_in(x2d, norm, w, conv_w, conv_b, mk, mv, q_gain, seq):
    t, d = x2d.shape
    conv_width = conv_w.shape[1]
    n_conv = conv_width // CONV_CW
    mem_cols = 2 * MEM_WIDTH
    assert w.shape[1] == 4 * conv_width + mem_cols and (4 * conv_width) % mem_cols == 0
    tiles_per_seq = seq // TM
    halo_blocks = TM // HALO
    kern = functools.partial(_a_in_kernel, tiles_per_seq=tiles_per_seq, n_conv=n_conv)
    last = n_conv - 1

    def conv_cols(part):
        return pl.BlockSpec((d, CONV_CW), lambda i, c: (0, part * n_conv + jnp.minimum(c, last)))

    return pl.pallas_call(
        kern,
        grid=(t // TM, n_conv + 1),
        in_specs=[
            pl.BlockSpec((TM, d), lambda i, c: (i, 0)),
            pl.BlockSpec((HALO, d), lambda i, c: (jnp.maximum(i * halo_blocks - 1, 0), 0)),
            pl.BlockSpec((1, d), lambda i, c: (0, 0)),
            conv_cols(0), conv_cols(1), conv_cols(2), conv_cols(3),
            pl.BlockSpec((d, mem_cols), lambda i, c: (0, 4 * conv_width // mem_cols)),
            pl.BlockSpec((CONV_SIZE, CONV_CW), lambda i, c: (0, jnp.minimum(c, last))),
            pl.BlockSpec((1, CONV_CW), lambda i, c: (0, jnp.minimum(c, last))),
            pl.BlockSpec((MEM_TOKENS, MEM_WIDTH), lambda i, c: (i // tiles_per_seq, 0)),
            pl.BlockSpec((MEM_TOKENS, MEM_WIDTH), lambda i, c: (i // tiles_per_seq, 0)),
            pl.BlockSpec((1, HEAD_DIM), lambda i, c: (0, 0)),
        ],
        out_specs=[
            pl.BlockSpec((TM, CONV_CW), lambda i, c: (i, jnp.minimum(c, last))),
            pl.BlockSpec((TM, MEM_WIDTH), lambda i, c: (i, 0)),
        ],
        out_shape=[jax.ShapeDtypeStruct((t, conv_width), BF16),
                   jax.ShapeDtypeStruct((t, MEM_WIDTH), BF16)],
        scratch_shapes=[pltpu.VMEM((TM + HALO, d), BF16)],
        compiler_params=_params(56, 2),
        name="a_in",
    )(x2d, x2d, norm[None, :], w, w, w, w, w, conv_w, conv_b[None, :], mk, mv, q_gain[None, :])


def _out_proj_kernel(res_ref, y1_ref, y2_ref, w1_ref, w2_ref, o_ref):
    o_ref[...] = res_ref[...] + _dot(y1_ref[...], w1_ref[...]) + _dot(y2_ref[...], w2_ref[...])


def _out_proj(res, y1, y2, w):
    t, d = res.shape
    tm = TM_RESIDENT
    n1, n2 = y1.shape[1], y2.shape[1]
    assert n1 % n2 == 0 and n1 + n2 == w.shape[0]
    once = pl.Buffered(1)
    return pl.pallas_call(
        _out_proj_kernel,
        grid=(t // tm,),
        in_specs=[
            pl.BlockSpec((tm, d), lambda i: (i, 0)),
            pl.BlockSpec((tm, n1), lambda i: (i, 0)),
            pl.BlockSpec((tm, n2), lambda i: (i, 0)),
            pl.BlockSpec((n1, d), lambda i: (0, 0), pipeline_mode=once),
            pl.BlockSpec((n2, d), lambda i: (n1 // n2, 0), pipeline_mode=once),
        ],
        out_specs=pl.BlockSpec((tm, d), lambda i: (i, 0)),
        out_shape=jax.ShapeDtypeStruct((t, d), F32),
        compiler_params=_params(48, 1),
        name="out_proj",
    )(res, y1, y2, w, w)


def _store_transposed(dst_ref, val):
    vt = val.T.astype(BF16)
    for g in range(NSA_KV_HEADS):
        for cc in range(val.shape[0] // KC):
            dst_ref[0, g, cc] = vt[g * HEAD_DIM:(g + 1) * HEAD_DIM, cc * KC:(cc + 1) * KC]


def _nsa_kv_kernel(x_ref, g_ref, w_ref, kns_ref, knw_ref,
                   kc_ref, vc_ref, ks_ref, vst_ref, kw_ref, vwt_ref):
    bw = KV_BRANCH_WIDTH
    h = _rms_rows(x_ref[...], g_ref[...]).astype(BF16)

    def branch(j):
        return _dot(h, w_ref[:, j * bw:(j + 1) * bw])

    def normed_keys(acc, gain):
        return jnp.concatenate(
            [_rms_rows(acc[:, g * HEAD_DIM:(g + 1) * HEAD_DIM], gain).astype(BF16)
             for g in range(NSA_KV_HEADS)], axis=1)

    kc_ref[...] = branch(0)
    vc_ref[...] = branch(1)
    ks_ref[...] = normed_keys(branch(2), kns_ref[...])
    _store_transposed(vst_ref, branch(3))
    kw_ref[...] = normed_keys(branch(4), knw_ref[...])
    _store_transposed(vwt_ref, branch(5))


def _nsa_kv(x2d, norm, w, kn_slc, kn_win, batch, seq):
    t, d = x2d.shape
    bw = KV_BRANCH_WIDTH
    tm = TM_RESIDENT
    tiles_per_seq = seq // tm
    row = lambda i: (i, 0)
    const = lambda i: (0, 0)
    vt_spec = pl.BlockSpec((1, NSA_KV_HEADS, tm // KC, HEAD_DIM, KC),
                           lambda i: (i // tiles_per_seq, 0, i % tiles_per_seq, 0, 0))
    vt_shape = jax.ShapeDtypeStruct((batch, NSA_KV_HEADS, seq // KC, HEAD_DIM, KC), BF16)
    return pl.pallas_call(
        _nsa_kv_kernel,
        grid=(t // tm,),
        in_specs=[
            pl.BlockSpec((tm, d), row),
            pl.BlockSpec((1, d), const),
            pl.BlockSpec((d, 6 * bw), const, pipeline_mode=pl.Buffered(1)),
            pl.BlockSpec((1, HEAD_DIM), const),
            pl.BlockSpec((1, HEAD_DIM), const),
        ],
        out_specs=[pl.BlockSpec((tm, bw), row), pl.BlockSpec((tm, bw), row),
                   pl.BlockSpec((tm, bw), row), vt_spec,
                   pl.BlockSpec((tm, bw), row), vt_spec],
        out_shape=[jax.ShapeDtypeStruct((t, bw), F32), jax.ShapeDtypeStruct((t, bw), F32),
                   jax.ShapeDtypeStruct((t, bw), BF16), vt_shape,
                   jax.ShapeDtypeStruct((t, bw), BF16), vt_shape],
        compiler_params=_params(44, 1),
        name="nsa_kv",
    )(x2d, norm[None, :], w, kn_slc[None, :], kn_win[None, :])


def _compress_kernel(*refs, is_key):
    x_refs = refs[:NSA_KV_HEADS]
    pos_ref, w1_ref, w2_ref, kn_ref, o_ref = refs[NSA_KV_HEADS:]
    n_chunks = x_refs[0].shape[0] // CMP_STRIDE
    pairs = CMP_STRIDE // 2
    first = second = None
    for p in range(pairs):
        lhs_a, lhs_b = [], []
        for l in (2 * p, 2 * p + 1):
            xg = jnp.concatenate([x_ref[pl.ds(l, n_chunks, stride=CMP_STRIDE), :]
                                  for x_ref in x_refs], axis=0)
            lhs_a.append((xg + pos_ref[l:l + 1, :]).astype(BF16))
            lhs_b.append((xg + pos_ref[CMP_STRIDE + l:CMP_STRIDE + l + 1, :]).astype(BF16))
        da = _dot(jnp.concatenate(lhs_a, axis=1), w1_ref[p])
        db = _dot(jnp.concatenate(lhs_b, axis=1), w1_ref[pairs + p])
        first = da if first is None else first + da
        second = db if second is None else second + db
    rows = first.shape[0]
    hid = _silu(first + pltpu.roll(second, rows - 1, axis=0))
    o = _dot(hid.astype(BF16), w2_ref[...])
    if is_key:
        o = _rms_rows(o, kn_ref[...])
    chunk = lax.broadcasted_iota(jnp.int32, o.shape, 0) & (n_chunks - 1)
    o = jnp.where(chunk == n_chunks - 1, 0.0, o)
    if is_key:
        o_ref[...] = o.astype(BF16)
    else:
        for g in range(NSA_KV_HEADS):
            o_ref[g] = o[g * n_chunks:(g + 1) * n_chunks, :].T.astype(BF16)


def _compress(t2d, pos, w1, w2, kn, is_key, batch, seq):
    assert t2d.shape[1] == NSA_KV_HEADS * HEAD_DIM
    n_chunks = seq // CMP_STRIDE
    rows = NSA_KV_HEADS * n_chunks
    w1p = w1.reshape(CMP_BLOCK // 2, 2 * HEAD_DIM, CMP_HIDDEN).astype(BF16)
    kern = functools.partial(_compress_kernel, is_key=is_key)
    if is_key:
        out_spec = pl.BlockSpec((rows, HEAD_DIM), lambda b: (b, 0))
        out_shape = jax.ShapeDtypeStruct((batch * rows, HEAD_DIM), BF16)
    else:
        out_spec = pl.BlockSpec((NSA_KV_HEADS, HEAD_DIM, n_chunks), lambda b: (b, 0, 0))
        out_shape = jax.ShapeDtypeStruct((batch * NSA_KV_HEADS, HEAD_DIM, n_chunks), BF16)
    return pl.pallas_call(
        kern,
        grid=(batch,),
        in_specs=[pl.BlockSpec((seq, HEAD_DIM), functools.partial(lambda b, g: (b, g), g=g))
                  for g in range(NSA_KV_HEADS)] + [
            pl.BlockSpec((CMP_BLOCK, HEAD_DIM), lambda b: (0, 0)),
            pl.BlockSpec((CMP_BLOCK // 2, 2 * HEAD_DIM, CMP_HIDDEN), lambda b: (0, 0, 0)),
            pl.BlockSpec((CMP_HIDDEN, HEAD_DIM), lambda b: (0, 0)),
            pl.BlockSpec((1, HEAD_DIM), lambda b: (0, 0)),
        ],
        out_specs=out_spec,
        out_shape=out_shape,
        compiler_params=_params(32, 1),
        name="compress_k" if is_key else "compress_v",
    )(*([t2d] * NSA_KV_HEADS), pos, w1p, w2.astype(BF16), kn[None, :])


def _b_in_kernel(x_ref, g_ref, w_ref, wg_ref, gb_ref, qg_ref, mk_ref, mv_ref, mqg_ref,
                 qt_ref, gate_ref, sz_ref, ym_ref):
    tm = x_ref.shape[0]
    h = _rms_rows(x_ref[...], g_ref[...]).astype(BF16)
    gate_ref[...] = jax.nn.sigmoid(_dot(h, wg_ref[...]) + gb_ref[...])

    def cols(start, width):
        return _dot(h, w_ref[:, start:start + width])

    step = 4 * HEAD_DIM
    gain = jnp.tile(qg_ref[...], (1, tm // HEAD_DIM))
    for c in range(NSA_WIDTH // step):
        acc = cols(c * step, step)
        for hd in range(step // HEAD_DIM):
            qt = acc[:, hd * HEAD_DIM:(hd + 1) * HEAD_DIM].T
            inv = lax.rsqrt(jnp.mean(qt * qt, axis=0, keepdims=True) + EPS)
            row0 = c * step + hd * HEAD_DIM
            qt_ref[0, row0:row0 + HEAD_DIM, :] = (qt * inv * gain).astype(BF16)
    for c in range(NSA_WIDTH // step):
        sz_ref[:, c * step:(c + 1) * step] = _silu(cols(NSA_WIDTH + c * step, step)).astype(BF16)
    outs = _mem_attention(cols(2 * NSA_WIDTH, MEM_WIDTH), mk_ref, mv_ref, mqg_ref[...])
    mz = cols(2 * NSA_WIDTH + MEM_WIDTH, MEM_WIDTH)
    for hd in range(MEM_HEADS):
        sl = slice(hd * HEAD_DIM, (hd + 1) * HEAD_DIM)
        ym_ref[:, sl] = (outs[hd] * _silu(mz[:, sl])).astype(BF16)


def _b_in(x2d, norm, w_main, w_gate, gate_bias, q_gain_b, mk, mv, mem_q_gain, batch, seq):
    t, d = x2d.shape
    tm = TM_RESIDENT
    tiles_per_seq = seq // tm
    const = lambda i: (0, 0)
    row = lambda i: (i, 0)
    mem_idx = lambda i: (i // tiles_per_seq, 0)
    once = pl.Buffered(1)
    return pl.pallas_call(
        _b_in_kernel,
        grid=(t // tm,),
        in_specs=[
            pl.BlockSpec((tm, d), row),
            pl.BlockSpec((1, d), const),
            pl.BlockSpec(w_main.shape, const, pipeline_mode=once),
            pl.BlockSpec((d, V7X_LANES), const, pipeline_mode=once),
            pl.BlockSpec((1, V7X_LANES), const),
            pl.BlockSpec((HEAD_DIM, HEAD_DIM), const),
            pl.BlockSpec((MEM_TOKENS, MEM_WIDTH), mem_idx),
            pl.BlockSpec((MEM_TOKENS, MEM_WIDTH), mem_idx),
            pl.BlockSpec((1, HEAD_DIM), const),
        ],
        out_specs=[
            pl.BlockSpec((1, NSA_WIDTH, tm), lambda i: (i // tiles_per_seq, 0, i % tiles_per_seq)),
            pl.BlockSpec((tm, V7X_LANES), row),
            pl.BlockSpec((tm, NSA_WIDTH), row),
            pl.BlockSpec((tm, MEM_WIDTH), row),
        ],
        out_shape=[jax.ShapeDtypeStruct((batch, NSA_WIDTH, seq), BF16),
                   jax.ShapeDtypeStruct((t, V7X_LANES), F32),
                   jax.ShapeDtypeStruct((t, NSA_WIDTH), BF16),
                   jax.ShapeDtypeStruct((t, MEM_WIDTH), BF16)],
        compiler_params=_params(48, 1),
        name="b_in",
    )(x2d, norm[None, :], w_main, w_gate, gate_bias, q_gain_b, mk, mv, mem_q_gain[None, :])


def _attn_block(qi, fast, row0, q3, gates, stab, kcmp_ref, vcmpt_ref, ks_ref, vst_ref, kw_ref, vwt_ref,
                sz_ref, cover_ref, onehot_ref, y_ref, ssel_ref, swin_ref):
    n = NSA_GROUP * TQ
    qs = qi * TQ
    par = qi % 2
    q = [q3[j * HEAD_DIM:(j + 1) * HEAD_DIM] for j in range(NSA_GROUP)]
    k_loc = lax.broadcasted_iota(jnp.int32, (KC, TQ), 0)
    t_loc = lax.broadcasted_iota(jnp.int32, (KC, TQ), 1)
    causal = k_loc <= t_loc
    band = k_loc > t_loc
    sel_chunks = range(qi + 1)
    win_chunks = range(max(qi - WINDOW // KC, 0), qi + 1)
    n_allowed = (qs + TQ - 1) // SLC_BLOCK + 1
    use_bias = n_allowed > SLC_TOPK

    ones_rows = jnp.ones((BF16_SUBLANES, KC), BF16)

    def scores_pass(is_sel, j, sel_bias=None):
        k_ref, s_ref = (ks_ref, ssel_ref) if is_sel else (kw_ref, swin_ref)
        qj = q[j]
        augmented = fast or sel_bias is not None
        if augmented:
            extra = jnp.zeros((HEAD_DIM, TQ), F32) if sel_bias is None else jnp.concatenate(
                [sel_bias, jnp.zeros((HEAD_DIM - sel_bias.shape[0], TQ), F32)], axis=0)
            if fast:
                row = lax.broadcasted_iota(jnp.int32, (HEAD_DIM, TQ), 0)
                extra = jnp.where(row == STAB_ROW, -stab, extra)
            qj = jnp.concatenate([qj, extra.astype(BF16)], axis=0)
        m = None
        for i, ck in enumerate(sel_chunks if is_sel else win_chunks):
            k = k_ref[ck * KC:(ck + 1) * KC, :]
            if augmented:
                k = jnp.concatenate([k, onehot_ref[ck * KC:(ck + 1) * KC, :]], axis=1)
            sc = _dot(k, qj)
            if not is_sel and ck == qi - WINDOW // KC:
                sc = jnp.where(band, sc, NEG)
            if ck == qi:
                sc = jnp.where(causal, sc, NEG)
            s_ref[par, j, i * KC:(i + 1) * KC, :] = sc
            if not fast:
                m_c = jnp.max(sc, axis=0, keepdims=True)
                m = m_c if m is None else jnp.maximum(m, m_c)
        return m

    def values_pass(is_sel, j, m):
        vt_ref, s_ref = (vst_ref, ssel_ref) if is_sel else (vwt_ref, swin_ref)
        l = acc = None
        for i, ck in enumerate(sel_chunks if is_sel else win_chunks):
            sc = s_ref[par, j, i * KC:(i + 1) * KC, :]
            vt = vt_ref[0, 0, ck]
            if fast:
                pe = jnp.exp2(sc)
                vt = jnp.concatenate([vt, ones_rows], axis=0)
            else:
                pe = jnp.exp2(sc - m)
                l_c = jnp.sum(pe, axis=0, keepdims=True)
                l = l_c if l is None else l + l_c
            pv = _dot(vt, pe.astype(BF16))
            acc = pv if acc is None else acc + pv
        if fast:
            acc, l = acc[:HEAD_DIM], acc[HEAD_DIM:HEAD_DIM + 1]
        return acc, l

    def merge(j, o_cmp_j, sel_j, win_j):
        yt = (gates[3 * j:3 * j + 1] * o_cmp_j
              + (gates[3 * j + 1:3 * j + 2] / sel_j[1]) * sel_j[0]
              + (gates[3 * j + 2:3 * j + 3] / win_j[1]) * win_j[0])
        ch = slice(j * HEAD_DIM, (j + 1) * HEAD_DIM)
        rows = slice(row0, row0 + TQ)
        y_ref[rows, ch] = (yt.T * sz_ref[rows, ch].astype(F32)).astype(BF16)

    n_cmp = kcmp_ref.shape[0]
    n_vis = min(n_cmp, CMP_PER_TILE * (qi + 1))
    s_cmp = _dot(kcmp_ref[0:n_vis, :], jnp.concatenate(q, axis=1))
    m_win = [scores_pass(False, 0)]
    top = max(n_vis - 2 * CMP_PER_TILE, 0)
    c_idx = top + lax.broadcasted_iota(jnp.int32, (n_vis - top, n), 0)
    t_idx = qs + (lax.broadcasted_iota(jnp.int32, (n_vis - top, n), 1) & (TQ - 1))
    slab_ok = c_idx * CMP_STRIDE + (CMP_BLOCK - 1) <= t_idx

    def with_slab(v, fill):
        slab = jnp.where(slab_ok, v[top:], fill)
        return slab if top == 0 else jnp.concatenate([v[:top], slab], axis=0)

    sm = with_slab(s_cmp, NEG)
    e = jnp.exp2(sm - jnp.max(sm, axis=0, keepdims=True))
    p = with_slab(e / jnp.sum(e, axis=0, keepdims=True), 0.0)
    if n_vis < n_cmp:
        p = jnp.concatenate([p, jnp.zeros((n_cmp - n_vis, n), F32)], axis=0)
    o_cmp = _dot(vcmpt_ref[0], p.astype(BF16))
    m_win.append(scores_pass(False, 1))

    sel_bias = None
    if use_bias:
        imp = p[:, 0:TQ] + p[:, TQ:2 * TQ] + p[:, 2 * TQ:3 * TQ]
        hi = imp.astype(BF16)
        r1 = imp - hi.astype(F32)
        mid = r1.astype(BF16)
        lo = (r1 - mid.astype(F32)).astype(BF16)
        cover = cover_ref[...]
        score = _dot(cover, hi) + _dot(cover, mid) + _dot(cover, lo)
        n_sel = score.shape[0]
        j_idx = lax.broadcasted_iota(jnp.int32, (n_sel, TQ), 0)
        tq_idx = qs + lax.broadcasted_iota(jnp.int32, (n_sel, TQ), 1)
        cur = tq_idx >> SLC_SHIFT
        allowed = j_idx * SLC_BLOCK <= tq_idx
        forced = (j_idx == 0) | (j_idx == cur) | (j_idx == cur - 1)
        score = jnp.where(forced, FORCE, jnp.where(allowed, score, NEG))
        rank = jnp.zeros((n_sel, TQ), jnp.int32)
        for jp in range(n_allowed):
            row = score[jp:jp + 1, :]
            before = (row > score) | ((row == score) & (j_idx > jp))
            rank = rank + before.astype(jnp.int32)
        sel_bias = jnp.where(allowed & (rank < SLC_TOPK), 0.0, NEG)

    win = [values_pass(False, 0, m_win[0])]
    m_win.append(scores_pass(False, 2))
    win.append(values_pass(False, 1, m_win[1]))
    m_sel = [scores_pass(True, 0, sel_bias)]
    win.append(values_pass(False, 2, m_win[2]))
    for j in range(NSA_GROUP):
        if j + 1 < NSA_GROUP:
            m_sel.append(scores_pass(True, j + 1, sel_bias))
        merge(j, o_cmp[:, j * TQ:(j + 1) * TQ], values_pass(True, j, m_sel[j]), win[j])


def _nsa_attn_kernel(stab_ref, qt_ref, kcmp_ref, vcmpt_ref, ks_ref, vst_ref, kw_ref, vwt_ref,
                     gate_ref, sz_ref, cover_ref, onehot_ref, y_ref, gt_ref, ssel_ref, swin_ref):
    g = pl.program_id(1)
    step = pl.program_id(2)

    def tiles_body(first, fast):
        gt_ref[...] = gate_ref[...].T
        gates = gt_ref[pl.ds(pl.multiple_of(g * GATE_LANES, GATE_LANES), GATE_LANES), :]
        for r in range(TILES_PER_STEP):
            cols = slice(r * TQ, (r + 1) * TQ)
            _attn_block(first + r, fast, r * TQ, qt_ref[0, :, cols], gates[:, cols], stab,
                        kcmp_ref, vcmpt_ref, ks_ref, vst_ref, kw_ref, vwt_ref, sz_ref, cover_ref,
                        onehot_ref, y_ref, ssel_ref, swin_ref)

    stab = stab_ref[0]
    for s in range(ks_ref.shape[0] // (TQ * TILES_PER_STEP)):
        for fast in (True, False):
            pl.when((step == s) & ((stab <= STAB_MAX) == fast))(
                functools.partial(tiles_body, s * TILES_PER_STEP, fast))


def _nsa_attn(stab, qt, kcmp, vcmpt, ks, vst, kw, vwt, gates, sz, batch, seq):
    tq = TQ * TILES_PER_STEP
    n_q = seq // tq
    n_cmp = seq // CMP_STRIDE
    n_sel = seq // SLC_BLOCK
    n = NSA_GROUP * TQ
    i = np.arange(n_cmp)[:, None]
    j = np.arange(n_sel)[None, :]
    cover = ((i * CMP_STRIDE < (j + 1) * SLC_BLOCK) & (i * CMP_STRIDE + CMP_BLOCK > j * SLC_BLOCK))
    cover_t = jnp.asarray(cover.T, BF16)
    lane = np.arange(HEAD_DIM)[None, :]
    onehot = jnp.asarray((np.arange(seq)[:, None] // SLC_BLOCK == lane) | (lane == STAB_ROW), BF16)
    kv_spec = pl.BlockSpec((seq, HEAD_DIM), lambda b, g, q: (b, g))
    vt_spec = pl.BlockSpec((1, 1, seq // KC, HEAD_DIM, KC), lambda b, g, q: (b, g, 0, 0, 0))
    gw = NSA_GROUP * HEAD_DIM
    return pl.pallas_call(
        _nsa_attn_kernel,
        grid=(batch, NSA_KV_HEADS, n_q),
        in_specs=[
            pl.BlockSpec(memory_space=pltpu.SMEM),
            pl.BlockSpec((1, gw, tq), lambda b, g, q: (b, g, q)),
            pl.BlockSpec((n_cmp, HEAD_DIM), lambda b, g, q: (b * NSA_KV_HEADS + g, 0)),
            pl.BlockSpec((1, HEAD_DIM, n_cmp), lambda b, g, q: (b * NSA_KV_HEADS + g, 0, 0)),
            kv_spec, vt_spec, kv_spec, vt_spec,
            pl.BlockSpec((tq, V7X_LANES), lambda b, g, q: (b * n_q + q, 0)),
            pl.BlockSpec((tq, gw), lambda b, g, q: (b * n_q + q, g)),
            pl.BlockSpec((n_sel, n_cmp), lambda b, g, q: (0, 0)),
            pl.BlockSpec((seq, HEAD_DIM), lambda b, g, q: (0, 0)),
        ],
        out_specs=pl.BlockSpec((tq, gw), lambda b, g, q: (b * n_q + q, g)),
        out_shape=jax.ShapeDtypeStruct((batch * seq, NSA_WIDTH), BF16),
        scratch_shapes=[pltpu.VMEM((V7X_LANES, tq), F32),
                        pltpu.VMEM((2, NSA_GROUP, seq, TQ), F32),
                        pltpu.VMEM((2, NSA_GROUP, WINDOW + KC, TQ), F32)],
        compiler_params=_params(48, 3),
        name="nsa_attn",
    )(stab, qt, kcmp, vcmpt, ks, vst, kw, vwt, gates, sz, cover_t, onehot)


def _gate_layout(v):
    per_group = NSA_GROUP * 3
    v = v.reshape(v.shape[:-1] + (NSA_KV_HEADS, per_group))
    v = jnp.pad(v, [(0, 0)] * (v.ndim - 1) + [(0, GATE_LANES - per_group)])
    return v.reshape(v.shape[:-2] + (NSA_KV_HEADS * GATE_LANES,))


def kernel(x, mem, a_norm, a_w_in, a_conv_w, a_conv_b, a_w_out, kv_norm, kv_w,
           cmp_pos_k, cmp_w1_k, cmp_w2_k, cmp_pos_v, cmp_w1_v, cmp_w2_v,
           kn_cmp, kn_slc, kn_win, b_norm, b_w_in, b_gate_bias, b_q_norm, b_w_out,
           mem_norm, mem_w_kv, mem_q_norm, mem_k_norm):
    batch, seq, d = x.shape
    conv_width = d - MEM_WIDTH
    assert a_norm.shape[0] == 1 and b_norm.shape[0] == 1 and seq % TM == 0
    x2d = x.reshape(batch * seq, d)
    mem2d = mem.reshape(batch * MEM_TOKENS, d)

    mk, mv = _mem_kv(mem2d, mem_norm, mem_w_kv.astype(BF16), mem_k_norm)

    y_conv, y_mem = _a_in(x2d, a_norm[0], a_w_in[0].astype(BF16), a_conv_w[0], a_conv_b[0],
                          mk[0], mv[0], mem_q_norm[0], seq)
    x1 = _out_proj(x2d, y_conv, y_mem, a_w_out[0].astype(BF16))

    kc, vc, ks, vst, kw, vwt = _nsa_kv(x1, kv_norm, kv_w.astype(BF16), kn_slc, kn_win, batch, seq)
    k_cmp = _compress(kc, cmp_pos_k, cmp_w1_k, cmp_w2_k, kn_cmp, True, batch, seq)
    v_cmp_t = _compress(vc, cmp_pos_v, cmp_w1_v, cmp_w2_v, kn_cmp, False, batch, seq)

    wb = b_w_in[0]
    o = 0
    w_q = wb[:, o:o + NSA_WIDTH]; o += NSA_WIDTH
    w_g = wb[:, o:o + 3 * NSA_HEADS]; o += 3 * NSA_HEADS
    w_z = wb[:, o:o + NSA_WIDTH]; o += NSA_WIDTH
    w_m = wb[:, o:]
    w_main = jnp.concatenate([w_q, w_z, w_m], axis=1).astype(BF16)
    q_gain_b = jnp.broadcast_to((b_q_norm[0] * (SCALE * LOG2E))[:, None], (HEAD_DIM, HEAD_DIM))
    qt, gates, sz, y_mem2 = _b_in(x1, b_norm[0], w_main, _gate_layout(w_g).astype(BF16),
                                  _gate_layout(b_gate_bias[0])[None, :], q_gain_b,
                                  mk[1], mv[1], mem_q_norm[1], batch, seq)
    stab = (1.02 * HEAD_DIM * SCALE * LOG2E * jnp.max(jnp.abs(b_q_norm[0]))
            * jnp.maximum(jnp.max(jnp.abs(kn_slc)), jnp.max(jnp.abs(kn_win)))).reshape(1)
    y_nsa = _nsa_attn(stab, qt, k_cmp, v_cmp_t, ks, vst, kw, vwt, gates, sz, batch, seq)
    out = _out_proj(x1, y_nsa, y_mem2, b_w_out[0].astype(BF16))
    return out.reshape(batch, seq, d)
```

```python
import functools

import numpy as np
import jax
import jax.numpy as jnp
from jax import lax
from jax.experimental import pallas as pl
from jax.experimental.pallas import tpu as pltpu

HEAD_DIM = 128
MEM_TOKENS = 256
MEM_HEADS = 4
MEM_WIDTH = MEM_HEADS * HEAD_DIM
NSA_KV_HEADS = 4
NSA_GROUP = 3
NSA_HEADS = NSA_KV_HEADS * NSA_GROUP
NSA_WIDTH = NSA_HEADS * HEAD_DIM
KV_BRANCH_WIDTH = NSA_KV_HEADS * HEAD_DIM
CONV_SIZE = 3
CMP_BLOCK = 32
CMP_STRIDE = 16
CMP_HIDDEN = 256
SLC_BLOCK = 64
SLC_SHIFT = 6
SLC_TOPK = 16
WINDOW = 512
EPS = 1e-6
NEG = -1e30
FORCE = 1e4
SCALE = HEAD_DIM ** -0.5
LOG2E = 1.4426950408889634

V7X_VMEM_BYTES = 64 * 1024 * 1024
V7X_LANES = 128
BF16_SUBLANES = 16

TM = 1024
TM_RESIDENT = 512
HALO = BF16_SUBLANES
CONV_CW = 256
TQ = 256
TILES_PER_STEP = 8
KC = 256
GATE_LANES = 32
CMP_PER_TILE = TQ // CMP_STRIDE
STAB_ROW = 32
STAB_MAX = 56.0

BF16 = jnp.bfloat16
F32 = jnp.float32


def _params(vmem_mb, n_grid):
    return pltpu.CompilerParams(
        dimension_semantics=("arbitrary",) * n_grid,
        vmem_limit_bytes=vmem_mb * 1024 * 1024)


def _rms_rows(v, gain):
    return v * lax.rsqrt(jnp.mean(v * v, axis=-1, keepdims=True) + EPS) * gain


def _silu(v):
    return v * jax.nn.sigmoid(v)


def _dot(a, b):
    return jnp.dot(a, b, preferred_element_type=F32)


def _dot_nt(a, b):
    return lax.dot_general(a, b, (((1,), (1,)), ((), ())), preferred_element_type=F32)


def _mem_attention(mq, mk_ref, mv_ref, q_gain):
    outs = []
    for h in range(MEM_HEADS):
        sl = slice(h * HEAD_DIM, (h + 1) * HEAD_DIM)
        q = _rms_rows(mq[:, sl], q_gain * SCALE).astype(BF16)
        s = _dot_nt(q, mk_ref[:, sl])
        e = jnp.exp(s - jnp.max(s, axis=-1, keepdims=True))
        l = jnp.sum(e, axis=-1, keepdims=True)
        outs.append(_dot(e.astype(BF16), mv_ref[:, sl]) / l)
    return outs


def _mem_kv_kernel(mem_ref, g_ref, w_ref, kg_ref, mk_ref, mv_ref):
    h = _rms_rows(mem_ref[...], g_ref[0]).astype(BF16)
    kv = _dot(h, w_ref[0])
    for hd in range(MEM_HEADS):
        sl = slice(hd * HEAD_DIM, (hd + 1) * HEAD_DIM)
        mk_ref[0, :, sl] = _rms_rows(kv[:, sl], kg_ref[0]).astype(BF16)
    mv_ref[0] = kv[:, MEM_WIDTH:].astype(BF16)


def _mem_kv(mem2d, mem_norm, w_kv, k_gain):
    n_layers, d = mem_norm.shape
    rows = mem2d.shape[0]
    tm = min(TM, rows)
    out = jax.ShapeDtypeStruct((n_layers, rows, MEM_WIDTH), BF16)
    return pl.pallas_call(
        _mem_kv_kernel,
        grid=(n_layers, rows // tm),
        in_specs=[
            pl.BlockSpec((tm, d), lambda l, i: (i, 0)),
            pl.BlockSpec((1, 1, d), lambda l, i: (l, 0, 0)),
            pl.BlockSpec((1, d, 2 * MEM_WIDTH), lambda l, i: (l, 0, 0)),
            pl.BlockSpec((1, 1, HEAD_DIM), lambda l, i: (l, 0, 0)),
        ],
        out_specs=[pl.BlockSpec((1, tm, MEM_WIDTH), lambda l, i: (l, i, 0))] * 2,
        out_shape=[out, out],
        compiler_params=_params(40, 2),
        name="mem_kv",
    )(mem2d, mem_norm[:, None, :], w_kv, k_gain[:, None, :])


def _a_in_kernel(x_ref, xh_ref, g_ref, w_ref, cw_ref, cb_ref, mk_ref, mv_ref, qg_ref,
                 yc_ref, ym_ref, *, tiles_per_seq):
    conv_width = yc_ref.shape[1]
    keep = (pl.program_id(0) % tiles_per_seq != 0).astype(F32)
    h_halo = (_rms_rows(xh_ref[...], g_ref[...]) * keep).astype(BF16)
    h = _rms_rows(x_ref[...], g_ref[...]).astype(BF16)
    h_full = jnp.concatenate([h_halo, h], axis=0)
    for c in range(conv_width // CONV_CW):
        ch = slice(c * CONV_CW, (c + 1) * CONV_CW)

        def part(k, lhs):
            return _dot(lhs, w_ref[:, k * conv_width + c * CONV_CW:
                                   k * conv_width + (c + 1) * CONV_CW])

        u = part(1, h_full) * part(2, h_full)
        taps = cw_ref[:, ch]
        y = (taps[0:1] * pltpu.roll(u, 2, axis=0)[HALO:]
             + taps[1:2] * pltpu.roll(u, 1, axis=0)[HALO:]
             + taps[2:3] * u[HALO:])
        yc_ref[:, ch] = (part(0, h) * (y + cb_ref[:, ch]) * _silu(part(3, h))).astype(BF16)
    acc = _dot(h, w_ref[:, 4 * conv_width:])
    outs = _mem_attention(acc[:, :MEM_WIDTH], mk_ref, mv_ref, qg_ref[...])
    for hd in range(MEM_HEADS):
        sl = slice(hd * HEAD_DIM, (hd + 1) * HEAD_DIM)
        ym_ref[:, sl] = (outs[hd] * _silu(acc[:, MEM_WIDTH + hd * HEAD_DIM:
                                              MEM_WIDTH + (hd + 1) * HEAD_DIM])).astype(BF16)


def _a_in(x2d, norm, w, conv_w, conv_b, mk, mv, q_gain, seq):
    t, d = x2d.shape
    conv_width = conv_w.shape[1]
    assert w.shape[1] == 4 * conv_width + 2 * MEM_WIDTH and conv_width % CONV_CW == 0
    tm = TM_RESIDENT
    tiles_per_seq = seq // tm
    halo_blocks = tm // HALO
    kern = functools.partial(_a_in_kernel, tiles_per_seq=tiles_per_seq)
    const = lambda i: (0, 0)
    row = lambda i: (i, 0)
    mem_idx = lambda i: (i // tiles_per_seq, 0)
    return pl.pallas_call(
        kern,
        grid=(t // tm,),
        in_specs=[
            pl.BlockSpec((tm, d), row),
            pl.BlockSpec((HALO, d), lambda i: (jnp.maximum(i * halo_blocks - 1, 0), 0)),
            pl.BlockSpec((1, d), const),
            pl.BlockSpec(w.shape, const, pipeline_mode=pl.Buffered(1)),
            pl.BlockSpec((CONV_SIZE, conv_width), const),
            pl.BlockSpec((1, conv_width), const),
            pl.BlockSpec((MEM_TOKENS, MEM_WIDTH), mem_idx),
            pl.BlockSpec((MEM_TOKENS, MEM_WIDTH), mem_idx),
            pl.BlockSpec((1, HEAD_DIM), const),
        ],
        out_specs=[pl.BlockSpec((tm, conv_width), row), pl.BlockSpec((tm, MEM_WIDTH), row)],
        out_shape=[jax.ShapeDtypeStruct((t, conv_width), BF16),
                   jax.ShapeDtypeStruct((t, MEM_WIDTH), BF16)],
        compiler_params=_params(56, 1),
        name="a_in",
    )(x2d, x2d, norm[None, :], w, conv_w, conv_b[None, :], mk, mv, q_gain[None, :])


def _out_proj_kernel(res_ref, y1_ref, y2_ref, w1_ref, w2_ref, o_ref):
    o_ref[...] = res_ref[...] + _dot(y1_ref[...], w1_ref[...]) + _dot(y2_ref[...], w2_ref[...])


def _out_proj(res, y1, y2, w):
    t, d = res.shape
    tm = TM_RESIDENT
    n1, n2 = y1.shape[1], y2.shape[1]
    assert n1 % n2 == 0 and n1 + n2 == w.shape[0]
    once = pl.Buffered(1)
    return pl.pallas_call(
        _out_proj_kernel,
        grid=(t // tm,),
        in_specs=[
            pl.BlockSpec((tm, d), lambda i: (i, 0)),
            pl.BlockSpec((tm, n1), lambda i: (i, 0)),
            pl.BlockSpec((tm, n2), lambda i: (i, 0)),
            pl.BlockSpec((n1, d), lambda i: (0, 0), pipeline_mode=once),
            pl.BlockSpec((n2, d), lambda i: (n1 // n2, 0), pipeline_mode=once),
        ],
        out_specs=pl.BlockSpec((tm, d), lambda i: (i, 0)),
        out_shape=jax.ShapeDtypeStruct((t, d), F32),
        compiler_params=_params(48, 1),
        name="out_proj",
    )(res, y1, y2, w, w)


def _store_transposed(dst_ref, val):
    vt = val.T.astype(BF16)
    for g in range(NSA_KV_HEADS):
        for cc in range(val.shape[0] // KC):
            dst_ref[0, g, cc] = vt[g * HEAD_DIM:(g + 1) * HEAD_DIM, cc * KC:(cc + 1) * KC]


def _nsa_kv_kernel(x_ref, g_ref, w_ref, kns_ref, knw_ref,
                   kc_ref, vc_ref, ks_ref, vst_ref, kw_ref, vwt_ref):
    bw = KV_BRANCH_WIDTH
    h = _rms_rows(x_ref[...], g_ref[...]).astype(BF16)

    def branch(j):
        return _dot(h, w_ref[:, j * bw:(j + 1) * bw])

    def normed_keys(acc, gain):
        return jnp.concatenate(
            [_rms_rows(acc[:, g * HEAD_DIM:(g + 1) * HEAD_DIM], gain).astype(BF16)
             for g in range(NSA_KV_HEADS)], axis=1)

    kc_ref[...] = branch(0)
    vc_ref[...] = branch(1)
    ks_ref[...] = normed_keys(branch(2), kns_ref[...])
    _store_transposed(vst_ref, branch(3))
    kw_ref[...] = normed_keys(branch(4), knw_ref[...])
    _store_transposed(vwt_ref, branch(5))


def _nsa_kv(x2d, norm, w, kn_slc, kn_win, batch, seq):
    t, d = x2d.shape
    bw = KV_BRANCH_WIDTH
    tm = TM_RESIDENT
    tiles_per_seq = seq // tm
    row = lambda i: (i, 0)
    const = lambda i: (0, 0)
    vt_spec = pl.BlockSpec((1, NSA_KV_HEADS, tm // KC, HEAD_DIM, KC),
                           lambda i: (i // tiles_per_seq, 0, i % tiles_per_seq, 0, 0))
    vt_shape = jax.ShapeDtypeStruct((batch, NSA_KV_HEADS, seq // KC, HEAD_DIM, KC), BF16)
    return pl.pallas_call(
        _nsa_kv_kernel,
        grid=(t // tm,),
        in_specs=[
            pl.BlockSpec((tm, d), row),
            pl.BlockSpec((1, d), const),
            pl.BlockSpec((d, 6 * bw), const, pipeline_mode=pl.Buffered(1)),
            pl.BlockSpec((1, HEAD_DIM), const),
            pl.BlockSpec((1, HEAD_DIM), const),
        ],
        out_specs=[pl.BlockSpec((tm, bw), row), pl.BlockSpec((tm, bw), row),
                   pl.BlockSpec((tm, bw), row), vt_spec,
                   pl.BlockSpec((tm, bw), row), vt_spec],
        out_shape=[jax.ShapeDtypeStruct((t, bw), F32), jax.ShapeDtypeStruct((t, bw), F32),
                   jax.ShapeDtypeStruct((t, bw), BF16), vt_shape,
                   jax.ShapeDtypeStruct((t, bw), BF16), vt_shape],
        compiler_params=_params(44, 1),
        name="nsa_kv",
    )(x2d, norm[None, :], w, kn_slc[None, :], kn_win[None, :])


def _compress_kernel(*refs, is_key):
    x_refs = refs[:NSA_KV_HEADS]
    pos_ref, w1_ref, w2_ref, kn_ref, o_ref = refs[NSA_KV_HEADS:]
    n_chunks = x_refs[0].shape[0] // CMP_STRIDE
    pairs = CMP_STRIDE // 2
    first = second = None
    for p in range(pairs):
        lhs_a, lhs_b = [], []
        for l in (2 * p, 2 * p + 1):
            xg = jnp.concatenate([x_ref[pl.ds(l, n_chunks, stride=CMP_STRIDE), :]
                                  for x_ref in x_refs], axis=0)
            lhs_a.append((xg + pos_ref[l:l + 1, :]).astype(BF16))
            lhs_b.append((xg + pos_ref[CMP_STRIDE + l:CMP_STRIDE + l + 1, :]).astype(BF16))
        da = _dot(jnp.concatenate(lhs_a, axis=1), w1_ref[p])
        db = _dot(jnp.concatenate(lhs_b, axis=1), w1_ref[pairs + p])
        first = da if first is None else first + da
        second = db if second is None else second + db
    rows = first.shape[0]
    hid = _silu(first + pltpu.roll(second, rows - 1, axis=0))
    o = _dot(hid.astype(BF16), w2_ref[...])
    if is_key:
        o = _rms_rows(o, kn_ref[...])
    chunk = lax.broadcasted_iota(jnp.int32, o.shape, 0) & (n_chunks - 1)
    o = jnp.where(chunk == n_chunks - 1, 0.0, o)
    if is_key:
        o_ref[...] = o.astype(BF16)
    else:
        for g in range(NSA_KV_HEADS):
            o_ref[g] = o[g * n_chunks:(g + 1) * n_chunks, :].T.astype(BF16)


def _compress(t2d, pos, w1, w2, kn, is_key, batch, seq):
    assert t2d.shape[1] == NSA_KV_HEADS * HEAD_DIM
    n_chunks = seq // CMP_STRIDE
    rows = NSA_KV_HEADS * n_chunks
    w1p = w1.reshape(CMP_BLOCK // 2, 2 * HEAD_DIM, CMP_HIDDEN).astype(BF16)
    kern = functools.partial(_compress_kernel, is_key=is_key)
    if is_key:
        out_spec = pl.BlockSpec((rows, HEAD_DIM), lambda b: (b, 0))
        out_shape = jax.ShapeDtypeStruct((batch * rows, HEAD_DIM), BF16)
    else:
        out_spec = pl.BlockSpec((NSA_KV_HEADS, HEAD_DIM, n_chunks), lambda b: (b, 0, 0))
        out_shape = jax.ShapeDtypeStruct((batch * NSA_KV_HEADS, HEAD_DIM, n_chunks), BF16)
    return pl.pallas_call(
        kern,
        grid=(batch,),
        in_specs=[pl.BlockSpec((seq, HEAD_DIM), functools.partial(lambda b, g: (b, g), g=g))
                  for g in range(NSA_KV_HEADS)] + [
            pl.BlockSpec((CMP_BLOCK, HEAD_DIM), lambda b: (0, 0)),
            pl.BlockSpec((CMP_BLOCK // 2, 2 * HEAD_DIM, CMP_HIDDEN), lambda b: (0, 0, 0)),
            pl.BlockSpec((CMP_HIDDEN, HEAD_DIM), lambda b: (0, 0)),
            pl.BlockSpec((1, HEAD_DIM), lambda b: (0, 0)),
        ],
        out_specs=out_spec,
        out_shape=out_shape,
        compiler_params=_params(32, 1),
        name="compress_k" if is_key else "compress_v",
    )(*([t2d] * NSA_KV_HEADS), pos, w1p, w2.astype(BF16), kn[None, :])


def _b_in_kernel(x_ref, g_ref, w_ref, wg_ref, gb_ref, qg_ref, mk_ref, mv_ref, mqg_ref,
                 qt_ref, gate_ref, sz_ref, ym_ref):
    tm = x_ref.shape[0]
    h = _rms_rows(x_ref[...], g_ref[...]).astype(BF16)
    gate_ref[...] = jax.nn.sigmoid(_dot(h, wg_ref[...]) + gb_ref[...])

    def cols(start, width):
        return _dot(h, w_ref[:, start:start + width])

    step = 4 * HEAD_DIM
    gain = jnp.tile(qg_ref[...], (1, tm // HEAD_DIM))
    for c in range(NSA_WIDTH // step):
        acc = cols(c * step, step)
        for hd in range(step // HEAD_DIM):
            qt = acc[:, hd * HEAD_DIM:(hd + 1) * HEAD_DIM].T
            inv = lax.rsqrt(jnp.mean(qt * qt, axis=0, keepdims=True) + EPS)
            row0 = c * step + hd * HEAD_DIM
            qt_ref[0, row0:row0 + HEAD_DIM, :] = (qt * inv * gain).astype(BF16)
    for c in range(NSA_WIDTH // step):
        sz_ref[:, c * step:(c + 1) * step] = _silu(cols(NSA_WIDTH + c * step, step)).astype(BF16)
    outs = _mem_attention(cols(2 * NSA_WIDTH, MEM_WIDTH), mk_ref, mv_ref, mqg_ref[...])
    mz = cols(2 * NSA_WIDTH + MEM_WIDTH, MEM_WIDTH)
    for hd in range(MEM_HEADS):
        sl = slice(hd * HEAD_DIM, (hd + 1) * HEAD_DIM)
        ym_ref[:, sl] = (outs[hd] * _silu(mz[:, sl])).astype(BF16)


def _b_in(x2d, norm, w_main, w_gate, gate_bias, q_gain_b, mk, mv, mem_q_gain, batch, seq):
    t, d = x2d.shape
    tm = TM_RESIDENT
    tiles_per_seq = seq // tm
    const = lambda i: (0, 0)
    row = lambda i: (i, 0)
    mem_idx = lambda i: (i // tiles_per_seq, 0)
    once = pl.Buffered(1)
    return pl.pallas_call(
        _b_in_kernel,
        grid=(t // tm,),
        in_specs=[
            pl.BlockSpec((tm, d), row),
            pl.BlockSpec((1, d), const),
            pl.BlockSpec(w_main.shape, const, pipeline_mode=once),
            pl.BlockSpec((d, V7X_LANES), const, pipeline_mode=once),
            pl.BlockSpec((1, V7X_LANES), const),
            pl.BlockSpec((HEAD_DIM, HEAD_DIM), const),
            pl.BlockSpec((MEM_TOKENS, MEM_WIDTH), mem_idx),
            pl.BlockSpec((MEM_TOKENS, MEM_WIDTH), mem_idx),
            pl.BlockSpec((1, HEAD_DIM), const),
        ],
        out_specs=[
            pl.BlockSpec((1, NSA_WIDTH, tm), lambda i: (i // tiles_per_seq, 0, i % tiles_per_seq)),
            pl.BlockSpec((tm, V7X_LANES), row),
            pl.BlockSpec((tm, NSA_WIDTH), row),
            pl.BlockSpec((tm, MEM_WIDTH), row),
        ],
        out_shape=[jax.ShapeDtypeStruct((batch, NSA_WIDTH, seq), BF16),
                   jax.ShapeDtypeStruct((t, V7X_LANES), F32),
                   jax.ShapeDtypeStruct((t, NSA_WIDTH), BF16),
                   jax.ShapeDtypeStruct((t, MEM_WIDTH), BF16)],
        compiler_params=_params(48, 1),
        name="b_in",
    )(x2d, norm[None, :], w_main, w_gate, gate_bias, q_gain_b, mk, mv, mem_q_gain[None, :])


def _attn_block(qi, fast, row0, q3, gates, stab, kcmp_ref, vcmpt_ref, ks_ref, vst_ref, kw_ref, vwt_ref,
                sz_ref, cover_ref, onehot_ref, y_ref, ssel_ref, swin_ref):
    n = NSA_GROUP * TQ
    qs = qi * TQ
    par = qi % 2
    q = [q3[j * HEAD_DIM:(j + 1) * HEAD_DIM] for j in range(NSA_GROUP)]
    k_loc = lax.broadcasted_iota(jnp.int32, (KC, TQ), 0)
    t_loc = lax.broadcasted_iota(jnp.int32, (KC, TQ), 1)
    causal = k_loc <= t_loc
    band = k_loc > t_loc
    sel_chunks = range(qi + 1)
    win_chunks = range(max(qi - WINDOW // KC, 0), qi + 1)
    n_allowed = (qs + TQ - 1) // SLC_BLOCK + 1
    use_bias = n_allowed > SLC_TOPK

    ones_rows = jnp.ones((BF16_SUBLANES, KC), BF16)

    def scores_pass(is_sel, j, sel_bias=None):
        k_ref, s_ref = (ks_ref, ssel_ref) if is_sel else (kw_ref, swin_ref)
        qj = q[j]
        augmented = fast or sel_bias is not None
        if augmented:
            extra = jnp.zeros((HEAD_DIM, TQ), F32) if sel_bias is None else jnp.concatenate(
                [sel_bias, jnp.zeros((HEAD_DIM - sel_bias.shape[0], TQ), F32)], axis=0)
            if fast:
                row = lax.broadcasted_iota(jnp.int32, (HEAD_DIM, TQ), 0)
                extra = jnp.where(row == STAB_ROW, -stab, extra)
            qj = jnp.concatenate([qj, extra.astype(BF16)], axis=0)
        m = None
        for i, ck in enumerate(sel_chunks if is_sel else win_chunks):
            k = k_ref[ck * KC:(ck + 1) * KC, :]
            if augmented:
                k = jnp.concatenate([k, onehot_ref[ck * KC:(ck + 1) * KC, :]], axis=1)
            sc = _dot(k, qj)
            if not is_sel and ck == qi - WINDOW // KC:
                sc = jnp.where(band, sc, NEG)
            if ck == qi:
                sc = jnp.where(causal, sc, NEG)
            s_ref[par, j, i * KC:(i + 1) * KC, :] = sc
            if not fast:
                m_c = jnp.max(sc, axis=0, keepdims=True)
                m = m_c if m is None else jnp.maximum(m, m_c)
        return m

    def values_pass(is_sel, j, m):
        vt_ref, s_ref = (vst_ref, ssel_ref) if is_sel else (vwt_ref, swin_ref)
        l = acc = None
        for i, ck in enumerate(sel_chunks if is_sel else win_chunks):
            sc = s_ref[par, j, i * KC:(i + 1) * KC, :]
            vt = vt_ref[0, 0, ck]
            if fast:
                pe = jnp.exp2(sc)
                vt = jnp.concatenate([vt, ones_rows], axis=0)
            else:
                pe = jnp.exp2(sc - m)
                l_c = jnp.sum(pe, axis=0, keepdims=True)
                l = l_c if l is None else l + l_c
            pv = _dot(vt, pe.astype(BF16))
            acc = pv if acc is None else acc + pv
        if fast:
            acc, l = acc[:HEAD_DIM], acc[HEAD_DIM:HEAD_DIM + 1]
        return acc, l

    def merge(j, o_cmp_j, sel_j, win_j):
        yt = (gates[3 * j:3 * j + 1] * o_cmp_j
              + (gates[3 * j + 1:3 * j + 2] / sel_j[1]) * sel_j[0]
              + (gates[3 * j + 2:3 * j + 3] / win_j[1]) * win_j[0])
        ch = slice(j * HEAD_DIM, (j + 1) * HEAD_DIM)
        rows = slice(row0, row0 + TQ)
        y_ref[rows, ch] = (yt.T * sz_ref[rows, ch].astype(F32)).astype(BF16)

    n_cmp = kcmp_ref.shape[0]
    n_vis = min(n_cmp, CMP_PER_TILE * (qi + 1))
    s_cmp = _dot(kcmp_ref[0:n_vis, :], jnp.concatenate(q, axis=1))
    m_win = [scores_pass(False, 0)]
    top = max(n_vis - 2 * CMP_PER_TILE, 0)
    c_idx = top + lax.broadcasted_iota(jnp.int32, (n_vis - top, n), 0)
    t_idx = qs + (lax.broadcasted_iota(jnp.int32, (n_vis - top, n), 1) & (TQ - 1))
    slab_ok = c_idx * CMP_STRIDE + (CMP_BLOCK - 1) <= t_idx

    def with_slab(v, fill):
        slab = jnp.where(slab_ok, v[top:], fill)
        return slab if top == 0 else jnp.concatenate([v[:top], slab], axis=0)

    sm = with_slab(s_cmp, NEG)
    e = jnp.exp2(sm - jnp.max(sm, axis=0, keepdims=True))
    p = with_slab(e / jnp.sum(e, axis=0, keepdims=True), 0.0)
    if n_vis < n_cmp:
        p = jnp.concatenate([p, jnp.zeros((n_cmp - n_vis, n), F32)], axis=0)
    o_cmp = _dot(vcmpt_ref[0], p.astype(BF16))
    m_win.append(scores_pass(False, 1))

    sel_bias = None
    if use_bias:
        imp = p[:, 0:TQ] + p[:, TQ:2 * TQ] + p[:, 2 * TQ:3 * TQ]
        hi = imp.astype(BF16)
        r1 = imp - hi.astype(F32)
        mid = r1.astype(BF16)
        lo = (r1 - mid.astype(F32)).astype(BF16)
        cover = cover_ref[...]
        score = _dot(cover, hi) + _dot(cover, mid) + _dot(cover, lo)
        n_sel = score.shape[0]
        j_idx = lax.broadcasted_iota(jnp.int32, (n_sel, TQ), 0)
        tq_idx = qs + lax.broadcasted_iota(jnp.int32, (n_sel, TQ), 1)
        cur = tq_idx >> SLC_SHIFT
        allowed = j_idx * SLC_BLOCK <= tq_idx
        forced = (j_idx == 0) | (j_idx == cur) | (j_idx == cur - 1)
        score = jnp.where(forced, FORCE, jnp.where(allowed, score, NEG))
        rank = jnp.zeros((n_sel, TQ), jnp.int32)
        for jp in range(n_allowed):
            row = score[jp:jp + 1, :]
            before = (row > score) | ((row == score) & (j_idx > jp))
            rank = rank + before.astype(jnp.int32)
        sel_bias = jnp.where(allowed & (rank < SLC_TOPK), 0.0, NEG)

    win = [values_pass(False, 0, m_win[0])]
    m_win.append(scores_pass(False, 2))
    win.append(values_pass(False, 1, m_win[1]))
    m_sel = [scores_pass(True, 0, sel_bias)]
    win.append(values_pass(False, 2, m_win[2]))
    for j in range(NSA_GROUP):
        if j + 1 < NSA_GROUP:
            m_sel.append(scores_pass(True, j + 1, sel_bias))
        merge(j, o_cmp[:, j * TQ:(j + 1) * TQ], values_pass(True, j, m_sel[j]), win[j])


def _nsa_attn_kernel(stab_ref, qt_ref, kcmp_ref, vcmpt_ref, ks_ref, vst_ref, kw_ref, vwt_ref,
                     gate_ref, sz_ref, cover_ref, onehot_ref, y_ref, gt_ref, ssel_ref, swin_ref):
    g = pl.program_id(1)
    step = pl.program_id(2)

    def tiles_body(first, fast):
        gt_ref[...] = gate_ref[...].T
        gates = gt_ref[pl.ds(pl.multiple_of(g * GATE_LANES, GATE_LANES), GATE_LANES), :]
        for r in range(TILES_PER_STEP):
            cols = slice(r * TQ, (r + 1) * TQ)
            _attn_block(first + r, fast, r * TQ, qt_ref[0, :, cols], gates[:, cols], stab,
                        kcmp_ref, vcmpt_ref, ks_ref, vst_ref, kw_ref, vwt_ref, sz_ref, cover_ref,
                        onehot_ref, y_ref, ssel_ref, swin_ref)

    stab = stab_ref[0]
    for s in range(ks_ref.shape[0] // (TQ * TILES_PER_STEP)):
        for fast in (True, False):
            pl.when((step == s) & ((stab <= STAB_MAX) == fast))(
                functools.partial(tiles_body, s * TILES_PER_STEP, fast))


def _nsa_attn(stab, qt, kcmp, vcmpt, ks, vst, kw, vwt, gates, sz, batch, seq):
    tq = TQ * TILES_PER_STEP
    n_q = seq // tq
    n_cmp = seq // CMP_STRIDE
    n_sel = seq // SLC_BLOCK
    n = NSA_GROUP * TQ
    i = np.arange(n_cmp)[:, None]
    j = np.arange(n_sel)[None, :]
    cover = ((i * CMP_STRIDE < (j + 1) * SLC_BLOCK) & (i * CMP_STRIDE + CMP_BLOCK > j * SLC_BLOCK))
    cover_t = jnp.asarray(cover.T, BF16)
    lane = np.arange(HEAD_DIM)[None, :]
    onehot = jnp.asarray((np.arange(seq)[:, None] // SLC_BLOCK == lane) | (lane == STAB_ROW), BF16)
    kv_spec = pl.BlockSpec((seq, HEAD_DIM), lambda b, g, q: (b, g))
    vt_spec = pl.BlockSpec((1, 1, seq // KC, HEAD_DIM, KC), lambda b, g, q: (b, g, 0, 0, 0))
    gw = NSA_GROUP * HEAD_DIM
    return pl.pallas_call(
        _nsa_attn_kernel,
        grid=(batch, NSA_KV_HEADS, n_q),
        in_specs=[
            pl.BlockSpec(memory_space=pltpu.SMEM),
            pl.BlockSpec((1, gw, tq), lambda b, g, q: (b, g, q)),
            pl.BlockSpec((n_cmp, HEAD_DIM), lambda b, g, q: (b * NSA_KV_HEADS + g, 0)),
            pl.BlockSpec((1, HEAD_DIM, n_cmp), lambda b, g, q: (b * NSA_KV_HEADS + g, 0, 0)),
            kv_spec, vt_spec, kv_spec, vt_spec,
            pl.BlockSpec((tq, V7X_LANES), lambda b, g, q: (b * n_q + q, 0)),
            pl.BlockSpec((tq, gw), lambda b, g, q: (b * n_q + q, g)),
            pl.BlockSpec((n_sel, n_cmp), lambda b, g, q: (0, 0)),
            pl.BlockSpec((seq, HEAD_DIM), lambda b, g, q: (0, 0)),
        ],
        out_specs=pl.BlockSpec((tq, gw), lambda b, g, q: (b * n_q + q, g)),
        out_shape=jax.ShapeDtypeStruct((batch * seq, NSA_WIDTH), BF16),
        scratch_shapes=[pltpu.VMEM((V7X_LANES, tq), F32),
                        pltpu.VMEM((2, NSA_GROUP, seq, TQ), F32),
                        pltpu.VMEM((2, NSA_GROUP, WINDOW + KC, TQ), F32)],
        compiler_params=_params(48, 3),
        name="nsa_attn",
    )(stab, qt, kcmp, vcmpt, ks, vst, kw, vwt, gates, sz, cover_t, onehot)


def _repack_kernel(w_ref, main_ref, gate_ref):
    w = w_ref[...]
    n_gate = 3 * NSA_HEADS
    main_ref[:, :NSA_WIDTH] = w[:, :NSA_WIDTH].astype(BF16)
    main_ref[:, NSA_WIDTH:] = w[:, NSA_WIDTH + n_gate:].astype(BF16)
    per_group = n_gate // NSA_KV_HEADS
    pad = jnp.zeros((w.shape[0], GATE_LANES - per_group), F32)
    pieces = []
    for g in range(NSA_KV_HEADS):
        pieces += [w[:, NSA_WIDTH + g * per_group:NSA_WIDTH + (g + 1) * per_group], pad]
    gate_ref[...] = jnp.concatenate(pieces, axis=1).astype(BF16)


def _repack_b_w_in(w):
    d, n = w.shape
    n_main = n - 3 * NSA_HEADS
    tr = 256
    return pl.pallas_call(
        _repack_kernel,
        grid=(d // tr,),
        in_specs=[pl.BlockSpec((tr, n), lambda i: (i, 0))],
        out_specs=[pl.BlockSpec((tr, n_main), lambda i: (i, 0)),
                   pl.BlockSpec((tr, V7X_LANES), lambda i: (i, 0))],
        out_shape=[jax.ShapeDtypeStruct((d, n_main), BF16),
                   jax.ShapeDtypeStruct((d, V7X_LANES), BF16)],
        compiler_params=_params(32, 1),
        name="repack_b_w_in",
    )(w)


def _gate_layout(v):
    per_group = NSA_GROUP * 3
    v = v.reshape(v.shape[:-1] + (NSA_KV_HEADS, per_group))
    v = jnp.pad(v, [(0, 0)] * (v.ndim - 1) + [(0, GATE_LANES - per_group)])
    return v.reshape(v.shape[:-2] + (NSA_KV_HEADS * GATE_LANES,))


def kernel(x, mem, a_norm, a_w_in, a_conv_w, a_conv_b, a_w_out, kv_norm, kv_w,
           cmp_pos_k, cmp_w1_k, cmp_w2_k, cmp_pos_v, cmp_w1_v, cmp_w2_v,
           kn_cmp, kn_slc, kn_win, b_norm, b_w_in, b_gate_bias, b_q_norm, b_w_out,
           mem_norm, mem_w_kv, mem_q_norm, mem_k_norm):
    batch, seq, d = x.shape
    conv_width = d - MEM_WIDTH
    assert a_norm.shape[0] == 1 and b_norm.shape[0] == 1 and seq % (TQ * TILES_PER_STEP) == 0
    x2d = x.reshape(batch * seq, d)
    mem2d = mem.reshape(batch * MEM_TOKENS, d)

    mk, mv = _mem_kv(mem2d, mem_norm, mem_w_kv.astype(BF16), mem_k_norm)

    y_conv, y_mem = _a_in(x2d, a_norm[0], a_w_in[0].astype(BF16), a_conv_w[0], a_conv_b[0],
                          mk[0], mv[0], mem_q_norm[0], seq)
    x1 = _out_proj(x2d, y_conv, y_mem, a_w_out[0].astype(BF16))

    kc, vc, ks, vst, kw, vwt = _nsa_kv(x1, kv_norm, kv_w.astype(BF16), kn_slc, kn_win, batch, seq)
    k_cmp = _compress(kc, cmp_pos_k, cmp_w1_k, cmp_w2_k, kn_cmp, True, batch, seq)
    v_cmp_t = _compress(vc, cmp_pos_v, cmp_w1_v, cmp_w2_v, kn_cmp, False, batch, seq)

    w_main, w_gate = _repack_b_w_in(b_w_in[0])
    q_gain_b = jnp.broadcast_to((b_q_norm[0] * (SCALE * LOG2E))[:, None], (HEAD_DIM, HEAD_DIM))
    qt, gates, sz, y_mem2 = _b_in(x1, b_norm[0], w_main, w_gate,
                                  _gate_layout(b_gate_bias[0])[None, :], q_gain_b,
                                  mk[1], mv[1], mem_q_norm[1], batch, seq)
    stab = (1.02 * HEAD_DIM * SCALE * LOG2E * jnp.max(jnp.abs(b_q_norm[0]))
            * jnp.maximum(jnp.max(jnp.abs(kn_slc)), jnp.max(jnp.abs(kn_win)))).reshape(1)
    y_nsa = _nsa_attn(stab, qt, k_cmp, v_cmp_t, ks, vst, kw, vwt, gates, sz, batch, seq)
    out = _out_proj(x1, y_nsa, y_mem2, b_w_out[0].astype(BF16))
    return out.reshape(batch, seq, d)
```

```python
import functools

import numpy as np
import jax
import jax.numpy as jnp
from jax import lax
from jax.experimental import pallas as pl
from jax.experimental.pallas import tpu as pltpu

HEAD_DIM = 128
MEM_TOKENS = 256
MEM_HEADS = 4
MEM_WIDTH = MEM_HEADS * HEAD_DIM
NSA_KV_HEADS = 4
NSA_GROUP = 3
NSA_HEADS = NSA_KV_HEADS * NSA_GROUP
NSA_WIDTH = NSA_HEADS * HEAD_DIM
KV_BRANCH_WIDTH = NSA_KV_HEADS * HEAD_DIM
CONV_SIZE = 3
CMP_BLOCK = 32
CMP_STRIDE = 16
CMP_HIDDEN = 256
SLC_BLOCK = 64
SLC_SHIFT = 6
SLC_TOPK = 16
WINDOW = 512
EPS = 1e-6
NEG = -1e30
FORCE = 1e4
SCALE = HEAD_DIM ** -0.5
LOG2E = 1.4426950408889634

V7X_VMEM_BYTES = 64 * 1024 * 1024
V7X_LANES = 128
BF16_SUBLANES = 16

TM = 1024
TM_RESIDENT = 512
HALO = BF16_SUBLANES
CONV_CW = 256
TQ = 256
TILES_PER_STEP = 8
KC = 256
GATE_LANES = 32
CMP_PER_TILE = TQ // CMP_STRIDE
STAB_ROW = 32
STAB_MAX = 56.0

BF16 = jnp.bfloat16
F32 = jnp.float32


def _params(vmem_mb, n_grid):
    return pltpu.CompilerParams(
        dimension_semantics=("arbitrary",) * n_grid,
        vmem_limit_bytes=vmem_mb * 1024 * 1024)


def _rms_rows(v, gain):
    return v * lax.rsqrt(jnp.mean(v * v, axis=-1, keepdims=True) + EPS) * gain


def _silu(v):
    half = 0.5 * v
    return half + half * jnp.tanh(half)


def _dot(a, b):
    return jnp.dot(a, b, preferred_element_type=F32)


def _dot_nt(a, b):
    return lax.dot_general(a, b, (((1,), (1,)), ((), ())), preferred_element_type=F32)


def _mem_attention(mq, mk_ref, mv_ref, q_gain):
    outs = []
    for h in range(MEM_HEADS):
        sl = slice(h * HEAD_DIM, (h + 1) * HEAD_DIM)
        q = _rms_rows(mq[:, sl], q_gain * (SCALE * LOG2E)).astype(BF16)
        s = _dot_nt(q, mk_ref[:, sl])
        e = jnp.exp2(s - jnp.max(s, axis=-1, keepdims=True))
        l = jnp.sum(e, axis=-1, keepdims=True)
        outs.append(_dot(e.astype(BF16), mv_ref[:, sl]) / l)
    return outs


def _mem_kv_kernel(mem_ref, g_ref, w_ref, kg_ref, mk_ref, mv_ref):
    h = _rms_rows(mem_ref[...], g_ref[0]).astype(BF16)
    kv = _dot(h, w_ref[0])
    for hd in range(MEM_HEADS):
        sl = slice(hd * HEAD_DIM, (hd + 1) * HEAD_DIM)
        mk_ref[0, :, sl] = _rms_rows(kv[:, sl], kg_ref[0]).astype(BF16)
    mv_ref[0] = kv[:, MEM_WIDTH:].astype(BF16)


def _mem_kv(mem2d, mem_norm, w_kv, k_gain):
    n_layers, d = mem_norm.shape
    rows = mem2d.shape[0]
    tm = min(TM, rows)
    out = jax.ShapeDtypeStruct((n_layers, rows, MEM_WIDTH), BF16)
    return pl.pallas_call(
        _mem_kv_kernel,
        grid=(n_layers, rows // tm),
        in_specs=[
            pl.BlockSpec((tm, d), lambda l, i: (i, 0)),
            pl.BlockSpec((1, 1, d), lambda l, i: (l, 0, 0)),
            pl.BlockSpec((1, d, 2 * MEM_WIDTH), lambda l, i: (l, 0, 0)),
            pl.BlockSpec((1, 1, HEAD_DIM), lambda l, i: (l, 0, 0)),
        ],
        out_specs=[pl.BlockSpec((1, tm, MEM_WIDTH), lambda l, i: (l, i, 0))] * 2,
        out_shape=[out, out],
        compiler_params=_params(40, 2),
        name="mem_kv",
    )(mem2d, mem_norm[:, None, :], w_kv, k_gain[:, None, :])


def _a_in_kernel(x_ref, xh_ref, g_ref, w_ref, cw_ref, cb_ref, mk_ref, mv_ref, qg_ref,
                 yc_ref, ym_ref, *, tiles_per_seq):
    conv_width = yc_ref.shape[1]
    keep = (pl.program_id(0) % tiles_per_seq != 0).astype(F32)
    h_halo = (_rms_rows(xh_ref[...], g_ref[...]) * keep).astype(BF16)
    h = _rms_rows(x_ref[...], g_ref[...]).astype(BF16)
    h_full = jnp.concatenate([h_halo, h], axis=0)
    for c in range(conv_width // CONV_CW):
        ch = slice(c * CONV_CW, (c + 1) * CONV_CW)

        def part(k, lhs):
            return _dot(lhs, w_ref[:, k * conv_width + c * CONV_CW:
                                   k * conv_width + (c + 1) * CONV_CW])

        u = part(1, h_full) * part(2, h_full)
        taps = cw_ref[:, ch]
        y = (taps[0:1] * pltpu.roll(u, 2, axis=0)[HALO:]
             + taps[1:2] * pltpu.roll(u, 1, axis=0)[HALO:]
             + taps[2:3] * u[HALO:])
        yc_ref[:, ch] = (part(0, h) * (y + cb_ref[:, ch]) * _silu(part(3, h))).astype(BF16)
    acc = _dot(h, w_ref[:, 4 * conv_width:])
    outs = _mem_attention(acc[:, :MEM_WIDTH], mk_ref, mv_ref, qg_ref[...])
    for hd in range(MEM_HEADS):
        sl = slice(hd * HEAD_DIM, (hd + 1) * HEAD_DIM)
        ym_ref[:, sl] = (outs[hd] * _silu(acc[:, MEM_WIDTH + hd * HEAD_DIM:
                                              MEM_WIDTH + (hd + 1) * HEAD_DIM])).astype(BF16)


def _a_in(x2d, norm, w, conv_w, conv_b, mk, mv, q_gain, seq):
    t, d = x2d.shape
    conv_width = conv_w.shape[1]
    assert w.shape[1] == 4 * conv_width + 2 * MEM_WIDTH and conv_width % CONV_CW == 0
    tm = TM_RESIDENT
    tiles_per_seq = seq // tm
    halo_blocks = tm // HALO
    kern = functools.partial(_a_in_kernel, tiles_per_seq=tiles_per_seq)
    const = lambda i: (0, 0)
    row = lambda i: (i, 0)
    mem_idx = lambda i: (i // tiles_per_seq, 0)
    return pl.pallas_call(
        kern,
        grid=(t // tm,),
        in_specs=[
            pl.BlockSpec((tm, d), row),
            pl.BlockSpec((HALO, d), lambda i: (jnp.maximum(i * halo_blocks - 1, 0), 0)),
            pl.BlockSpec((1, d), const),
            pl.BlockSpec(w.shape, const, pipeline_mode=pl.Buffered(1)),
            pl.BlockSpec((CONV_SIZE, conv_width), const),
            pl.BlockSpec((1, conv_width), const),
            pl.BlockSpec((MEM_TOKENS, MEM_WIDTH), mem_idx),
            pl.BlockSpec((MEM_TOKENS, MEM_WIDTH), mem_idx),
            pl.BlockSpec((1, HEAD_DIM), const),
        ],
        out_specs=[pl.BlockSpec((tm, conv_width), row), pl.BlockSpec((tm, MEM_WIDTH), row)],
        out_shape=[jax.ShapeDtypeStruct((t, conv_width), BF16),
                   jax.ShapeDtypeStruct((t, MEM_WIDTH), BF16)],
        compiler_params=_params(56, 1),
        name="a_in",
    )(x2d, x2d, norm[None, :], w, conv_w, conv_b[None, :], mk, mv, q_gain[None, :])


def _out_proj_kernel(res_ref, y1_ref, y2_ref, w1_ref, w2_ref, o_ref):
    o_ref[...] = res_ref[...] + _dot(y1_ref[...], w1_ref[...]) + _dot(y2_ref[...], w2_ref[...])


def _out_proj(res, y1, y2, w):
    t, d = res.shape
    tm = TM
    n1, n2 = y1.shape[1], y2.shape[1]
    assert n1 % n2 == 0 and n1 + n2 == w.shape[0]
    once = pl.Buffered(1)
    return pl.pallas_call(
        _out_proj_kernel,
        grid=(t // tm,),
        in_specs=[
            pl.BlockSpec((tm, d), lambda i: (i, 0)),
            pl.BlockSpec((tm, n1), lambda i: (i, 0)),
            pl.BlockSpec((tm, n2), lambda i: (i, 0)),
            pl.BlockSpec((n1, d), lambda i: (0, 0), pipeline_mode=once),
            pl.BlockSpec((n2, d), lambda i: (n1 // n2, 0), pipeline_mode=once),
        ],
        out_specs=pl.BlockSpec((tm, d), lambda i: (i, 0)),
        out_shape=jax.ShapeDtypeStruct((t, d), F32),
        compiler_params=_params(57, 1),
        name="out_proj",
    )(res, y1, y2, w, w)


def _store_transposed(dst_ref, val):
    vt = val.T.astype(BF16)
    for g in range(NSA_KV_HEADS):
        for cc in range(val.shape[0] // KC):
            dst_ref[0, g, cc] = vt[g * HEAD_DIM:(g + 1) * HEAD_DIM, cc * KC:(cc + 1) * KC]


def _nsa_kv_kernel(x_ref, g_ref, w_ref, kns_ref, knw_ref,
                   kc_ref, vc_ref, ks_ref, vst_ref, kw_ref, vwt_ref):
    bw = KV_BRANCH_WIDTH
    h = _rms_rows(x_ref[...], g_ref[...]).astype(BF16)

    def branch(j):
        return _dot(h, w_ref[:, j * bw:(j + 1) * bw])

    def normed_keys(acc, gain):
        return jnp.concatenate(
            [_rms_rows(acc[:, g * HEAD_DIM:(g + 1) * HEAD_DIM], gain).astype(BF16)
             for g in range(NSA_KV_HEADS)], axis=1)

    kc_ref[...] = branch(0)
    vc_ref[...] = branch(1)
    ks_ref[...] = normed_keys(branch(2), kns_ref[...])
    _store_transposed(vst_ref, branch(3))
    kw_ref[...] = normed_keys(branch(4), knw_ref[...])
    _store_transposed(vwt_ref, branch(5))


def _nsa_kv(x2d, norm, w, kn_slc, kn_win, batch, seq):
    t, d = x2d.shape
    bw = KV_BRANCH_WIDTH
    tm = TM_RESIDENT
    tiles_per_seq = seq // tm
    row = lambda i: (i, 0)
    const = lambda i: (0, 0)
    vt_spec = pl.BlockSpec((1, NSA_KV_HEADS, tm // KC, HEAD_DIM, KC),
                           lambda i: (i // tiles_per_seq, 0, i % tiles_per_seq, 0, 0))
    vt_shape = jax.ShapeDtypeStruct((batch, NSA_KV_HEADS, seq // KC, HEAD_DIM, KC), BF16)
    return pl.pallas_call(
        _nsa_kv_kernel,
        grid=(t // tm,),
        in_specs=[
            pl.BlockSpec((tm, d), row),
            pl.BlockSpec((1, d), const),
            pl.BlockSpec((d, 6 * bw), const, pipeline_mode=pl.Buffered(1)),
            pl.BlockSpec((1, HEAD_DIM), const),
            pl.BlockSpec((1, HEAD_DIM), const),
        ],
        out_specs=[pl.BlockSpec((tm, bw), row), pl.BlockSpec((tm, bw), row),
                   pl.BlockSpec((tm, bw), row), vt_spec,
                   pl.BlockSpec((tm, bw), row), vt_spec],
        out_shape=[jax.ShapeDtypeStruct((t, bw), F32), jax.ShapeDtypeStruct((t, bw), F32),
                   jax.ShapeDtypeStruct((t, bw), BF16), vt_shape,
                   jax.ShapeDtypeStruct((t, bw), BF16), vt_shape],
        compiler_params=_params(44, 1),
        name="nsa_kv",
    )(x2d, norm[None, :], w, kn_slc[None, :], kn_win[None, :])


def _compress_kernel(*refs, is_key):
    x_refs = refs[:NSA_KV_HEADS]
    pos_ref, w1_ref, w2_ref, kn_ref, o_ref = refs[NSA_KV_HEADS:]
    n_chunks = x_refs[0].shape[0] // CMP_STRIDE
    pairs = CMP_STRIDE // 2
    first = second = None
    for p in range(pairs):
        lhs_a, lhs_b = [], []
        for l in (2 * p, 2 * p + 1):
            xg = jnp.concatenate([x_ref[pl.ds(l, n_chunks, stride=CMP_STRIDE), :]
                                  for x_ref in x_refs], axis=0)
            lhs_a.append((xg + pos_ref[l:l + 1, :]).astype(BF16))
            lhs_b.append((xg + pos_ref[CMP_STRIDE + l:CMP_STRIDE + l + 1, :]).astype(BF16))
        da = _dot(jnp.concatenate(lhs_a, axis=1), w1_ref[p])
        db = _dot(jnp.concatenate(lhs_b, axis=1), w1_ref[pairs + p])
        first = da if first is None else first + da
        second = db if second is None else second + db
    rows = first.shape[0]
    hid = _silu(first + pltpu.roll(second, rows - 1, axis=0))
    o = _dot(hid.astype(BF16), w2_ref[...])
    if is_key:
        o = _rms_rows(o, kn_ref[...])
    chunk = lax.broadcasted_iota(jnp.int32, o.shape, 0) & (n_chunks - 1)
    o = jnp.where(chunk == n_chunks - 1, 0.0, o)
    if is_key:
        o_ref[...] = o.astype(BF16)
    else:
        for g in range(NSA_KV_HEADS):
            o_ref[g] = o[g * n_chunks:(g + 1) * n_chunks, :].T.astype(BF16)


def _compress(t2d, pos, w1, w2, kn, is_key, batch, seq):
    assert t2d.shape[1] == NSA_KV_HEADS * HEAD_DIM
    n_chunks = seq // CMP_STRIDE
    rows = NSA_KV_HEADS * n_chunks
    w1p = w1.reshape(CMP_BLOCK // 2, 2 * HEAD_DIM, CMP_HIDDEN).astype(BF16)
    kern = functools.partial(_compress_kernel, is_key=is_key)
    if is_key:
        out_spec = pl.BlockSpec((rows, HEAD_DIM), lambda b: (b, 0))
        out_shape = jax.ShapeDtypeStruct((batch * rows, HEAD_DIM), BF16)
    else:
        out_spec = pl.BlockSpec((NSA_KV_HEADS, HEAD_DIM, n_chunks), lambda b: (b, 0, 0))
        out_shape = jax.ShapeDtypeStruct((batch * NSA_KV_HEADS, HEAD_DIM, n_chunks), BF16)
    return pl.pallas_call(
        kern,
        grid=(batch,),
        in_specs=[pl.BlockSpec((seq, HEAD_DIM), functools.partial(lambda b, g: (b, g), g=g))
                  for g in range(NSA_KV_HEADS)] + [
            pl.BlockSpec((CMP_BLOCK, HEAD_DIM), lambda b: (0, 0)),
            pl.BlockSpec((CMP_BLOCK // 2, 2 * HEAD_DIM, CMP_HIDDEN), lambda b: (0, 0, 0)),
            pl.BlockSpec((CMP_HIDDEN, HEAD_DIM), lambda b: (0, 0)),
            pl.BlockSpec((1, HEAD_DIM), lambda b: (0, 0)),
        ],
        out_specs=out_spec,
        out_shape=out_shape,
        compiler_params=_params(32, 1),
        name="compress_k" if is_key else "compress_v",
    )(*([t2d] * NSA_KV_HEADS), pos, w1p, w2.astype(BF16), kn[None, :])


def _b_in_kernel(x_ref, g_ref, w_ref, wg_ref, gb_ref, qg_ref, mk_ref, mv_ref, mqg_ref,
                 qt_ref, gate_ref, sz_ref, ym_ref):
    tm = x_ref.shape[0]
    h = _rms_rows(x_ref[...], g_ref[...]).astype(BF16)
    gate_ref[...] = jax.nn.sigmoid(_dot(h, wg_ref[...]) + gb_ref[...])

    def cols(start, width):
        return _dot(h, w_ref[:, start:start + width])

    step = 4 * HEAD_DIM
    gain = jnp.tile(qg_ref[...], (1, tm // HEAD_DIM))
    for c in range(NSA_WIDTH // step):
        acc = cols(c * step, step)
        for hd in range(step // HEAD_DIM):
            qt = acc[:, hd * HEAD_DIM:(hd + 1) * HEAD_DIM].T
            inv = lax.rsqrt(jnp.mean(qt * qt, axis=0, keepdims=True) + EPS)
            row0 = c * step + hd * HEAD_DIM
            qt_ref[0, row0:row0 + HEAD_DIM, :] = (qt * inv * gain).astype(BF16)
    for c in range(NSA_WIDTH // step):
        sz_ref[:, c * step:(c + 1) * step] = _silu(cols(NSA_WIDTH + c * step, step)).astype(BF16)
    outs = _mem_attention(cols(2 * NSA_WIDTH, MEM_WIDTH), mk_ref, mv_ref, mqg_ref[...])
    mz = cols(2 * NSA_WIDTH + MEM_WIDTH, MEM_WIDTH)
    for hd in range(MEM_HEADS):
        sl = slice(hd * HEAD_DIM, (hd + 1) * HEAD_DIM)
        ym_ref[:, sl] = (outs[hd] * _silu(mz[:, sl])).astype(BF16)


def _b_in(x2d, norm, w_main, w_gate, gate_bias, q_gain_b, mk, mv, mem_q_gain, batch, seq):
    t, d = x2d.shape
    tm = TM_RESIDENT
    tiles_per_seq = seq // tm
    const = lambda i: (0, 0)
    row = lambda i: (i, 0)
    mem_idx = lambda i: (i // tiles_per_seq, 0)
    once = pl.Buffered(1)
    return pl.pallas_call(
        _b_in_kernel,
        grid=(t // tm,),
        in_specs=[
            pl.BlockSpec((tm, d), row),
            pl.BlockSpec((1, d), const),
            pl.BlockSpec(w_main.shape, const, pipeline_mode=once),
            pl.BlockSpec((d, V7X_LANES), const, pipeline_mode=once),
            pl.BlockSpec((1, V7X_LANES), const),
            pl.BlockSpec((HEAD_DIM, HEAD_DIM), const),
            pl.BlockSpec((MEM_TOKENS, MEM_WIDTH), mem_idx),
            pl.BlockSpec((MEM_TOKENS, MEM_WIDTH), mem_idx),
            pl.BlockSpec((1, HEAD_DIM), const),
        ],
        out_specs=[
            pl.BlockSpec((1, NSA_WIDTH, tm), lambda i: (i // tiles_per_seq, 0, i % tiles_per_seq)),
            pl.BlockSpec((tm, V7X_LANES), row),
            pl.BlockSpec((tm, NSA_WIDTH), row),
            pl.BlockSpec((tm, MEM_WIDTH), row),
        ],
        out_shape=[jax.ShapeDtypeStruct((batch, NSA_WIDTH, seq), BF16),
                   jax.ShapeDtypeStruct((t, V7X_LANES), F32),
                   jax.ShapeDtypeStruct((t, NSA_WIDTH), BF16),
                   jax.ShapeDtypeStruct((t, MEM_WIDTH), BF16)],
        compiler_params=_params(48, 1),
        name="b_in",
    )(x2d, norm[None, :], w_main, w_gate, gate_bias, q_gain_b, mk, mv, mem_q_gain[None, :])


def _attn_block(qi, fast, row0, q3, gates, stab, kcmp_ref, vcmpt_ref, ks_ref, vst_ref, kw_ref, vwt_ref,
                sz_ref, cover_ref, onehot_ref, y_ref, ssel_ref, swin_ref):
    n = NSA_GROUP * TQ
    qs = qi * TQ
    par = qi % 2
    q = [q3[j * HEAD_DIM:(j + 1) * HEAD_DIM] for j in range(NSA_GROUP)]
    k_loc = lax.broadcasted_iota(jnp.int32, (KC, TQ), 0)
    t_loc = lax.broadcasted_iota(jnp.int32, (KC, TQ), 1)
    causal = k_loc <= t_loc
    band = k_loc > t_loc
    sel_chunks = range(qi + 1)
    win_chunks = range(max(qi - WINDOW // KC, 0), qi + 1)
    n_allowed = (qs + TQ - 1) // SLC_BLOCK + 1
    use_bias = n_allowed > SLC_TOPK

    def scores_pass(is_sel, j, sel_bias=None):
        k_ref, s_ref = (ks_ref, ssel_ref) if is_sel else (kw_ref, swin_ref)
        qj = q[j]
        augmented = fast or sel_bias is not None
        if augmented:
            extra = jnp.zeros((HEAD_DIM, TQ), F32) if sel_bias is None else jnp.concatenate(
                [sel_bias, jnp.zeros((HEAD_DIM - sel_bias.shape[0], TQ), F32)], axis=0)
            if fast:
                row = lax.broadcasted_iota(jnp.int32, (HEAD_DIM, TQ), 0)
                extra = jnp.where(row == STAB_ROW, -stab, extra)
            qj = jnp.concatenate([qj, extra.astype(BF16)], axis=0)
        m = None
        for i, ck in enumerate(sel_chunks if is_sel else win_chunks):
            k = k_ref[ck * KC:(ck + 1) * KC, :]
            if augmented:
                k = jnp.concatenate([k, onehot_ref[ck * KC:(ck + 1) * KC, :]], axis=1)
            sc = _dot(k, qj)
            if not is_sel and ck == qi - WINDOW // KC:
                sc = jnp.where(band, sc, NEG)
            if ck == qi:
                sc = jnp.where(causal, sc, NEG)
            s_ref[par, j, i * KC:(i + 1) * KC, :] = sc
            if not fast:
                m_c = jnp.max(sc, axis=0, keepdims=True)
                m = m_c if m is None else jnp.maximum(m, m_c)
        return m

    def values_pass(is_sel, j, m):
        vt_ref, s_ref = (vst_ref, ssel_ref) if is_sel else (vwt_ref, swin_ref)
        l = acc = None
        for i, ck in enumerate(sel_chunks if is_sel else win_chunks):
            sc = s_ref[par, j, i * KC:(i + 1) * KC, :]
            pe = jnp.exp2(sc if fast else sc - m)
            l_c = jnp.sum(pe, axis=0, keepdims=True)
            l = l_c if l is None else l + l_c
            pv = _dot(vt_ref[0, 0, ck], pe.astype(BF16))
            acc = pv if acc is None else acc + pv
        return acc, l

    def merge(j, o_cmp_j, sel_j, win_j):
        yt = (gates[3 * j:3 * j + 1] * o_cmp_j
              + (gates[3 * j + 1:3 * j + 2] / sel_j[1]) * sel_j[0]
              + (gates[3 * j + 2:3 * j + 3] / win_j[1]) * win_j[0])
        ch = slice(j * HEAD_DIM, (j + 1) * HEAD_DIM)
        rows = slice(row0, row0 + TQ)
        y_ref[rows, ch] = (yt.T * sz_ref[rows, ch].astype(F32)).astype(BF16)

    n_cmp = kcmp_ref.shape[0]
    n_vis = min(n_cmp, CMP_PER_TILE * (qi + 1))
    s_cmp = _dot(kcmp_ref[0:n_vis, :], jnp.concatenate(q, axis=1))
    m_win = [scores_pass(False, 0)]
    top = max(n_vis - 2 * CMP_PER_TILE, 0)
    c_idx = top + lax.broadcasted_iota(jnp.int32, (n_vis - top, n), 0)
    t_idx = qs + (lax.broadcasted_iota(jnp.int32, (n_vis - top, n), 1) & (TQ - 1))
    slab_ok = c_idx * CMP_STRIDE + (CMP_BLOCK - 1) <= t_idx

    def with_slab(v, fill):
        slab = jnp.where(slab_ok, v[top:], fill)
        return slab if top == 0 else jnp.concatenate([v[:top], slab], axis=0)

    sm = with_slab(s_cmp, NEG)
    e = jnp.exp2(sm - jnp.max(sm, axis=0, keepdims=True))
    p = with_slab(e / jnp.sum(e, axis=0, keepdims=True), 0.0)
    if n_vis < n_cmp:
        p = jnp.concatenate([p, jnp.zeros((n_cmp - n_vis, n), F32)], axis=0)
    o_cmp = _dot(vcmpt_ref[0], p.astype(BF16))
    m_win.append(scores_pass(False, 1))

    sel_bias = None
    if use_bias:
        imp = p[:, 0:TQ] + p[:, TQ:2 * TQ] + p[:, 2 * TQ:3 * TQ]
        hi = imp.astype(BF16)
        r1 = imp - hi.astype(F32)
        mid = r1.astype(BF16)
        lo = (r1 - mid.astype(F32)).astype(BF16)
        cover = cover_ref[...]
        score = _dot(cover, hi) + _dot(cover, mid) + _dot(cover, lo)
        n_sel = score.shape[0]
        j_idx = lax.broadcasted_iota(jnp.int32, (n_sel, TQ), 0)
        tq_idx = qs + lax.broadcasted_iota(jnp.int32, (n_sel, TQ), 1)
        cur = tq_idx >> SLC_SHIFT
        allowed = j_idx * SLC_BLOCK <= tq_idx
        forced = (j_idx == 0) | (j_idx == cur) | (j_idx == cur - 1)
        score = jnp.where(forced, FORCE, jnp.where(allowed, score, NEG))
        rank = jnp.zeros((n_sel, TQ), jnp.int32)
        for jp in range(n_allowed):
            row = score[jp:jp + 1, :]
            before = (row > score) | ((row == score) & (j_idx > jp))
            rank = rank + before.astype(jnp.int32)
        sel_bias = jnp.where(allowed & (rank < SLC_TOPK), 0.0, NEG)

    win = [values_pass(False, 0, m_win[0])]
    m_win.append(scores_pass(False, 2))
    win.append(values_pass(False, 1, m_win[1]))
    m_sel = [scores_pass(True, 0, sel_bias)]
    win.append(values_pass(False, 2, m_win[2]))
    for j in range(NSA_GROUP):
        if j + 1 < NSA_GROUP:
            m_sel.append(scores_pass(True, j + 1, sel_bias))
        merge(j, o_cmp[:, j * TQ:(j + 1) * TQ], values_pass(True, j, m_sel[j]), win[j])


def _nsa_attn_kernel(stab_ref, qt_ref, kcmp_ref, vcmpt_ref, ks_ref, vst_ref, kw_ref, vwt_ref,
                     gate_ref, sz_ref, cover_ref, onehot_ref, y_ref, gt_ref, ssel_ref, swin_ref):
    g = pl.program_id(1)
    step = pl.program_id(2)

    def tiles_body(first, fast):
        gt_ref[...] = gate_ref[...].T
        gates = gt_ref[pl.ds(pl.multiple_of(g * GATE_LANES, GATE_LANES), GATE_LANES), :]
        for r in range(TILES_PER_STEP):
            cols = slice(r * TQ, (r + 1) * TQ)
            _attn_block(first + r, fast, r * TQ, qt_ref[0, :, cols], gates[:, cols], stab,
                        kcmp_ref, vcmpt_ref, ks_ref, vst_ref, kw_ref, vwt_ref, sz_ref, cover_ref,
                        onehot_ref, y_ref, ssel_ref, swin_ref)

    stab = stab_ref[0]
    for s in range(ks_ref.shape[0] // (TQ * TILES_PER_STEP)):
        for fast in (True, False):
            pl.when((step == s) & ((stab <= STAB_MAX) == fast))(
                functools.partial(tiles_body, s * TILES_PER_STEP, fast))


def _nsa_attn(stab, qt, kcmp, vcmpt, ks, vst, kw, vwt, gates, sz, batch, seq):
    tq = TQ * TILES_PER_STEP
    n_q = seq // tq
    n_cmp = seq // CMP_STRIDE
    n_sel = seq // SLC_BLOCK
    n = NSA_GROUP * TQ
    i = np.arange(n_cmp)[:, None]
    j = np.arange(n_sel)[None, :]
    cover = ((i * CMP_STRIDE < (j + 1) * SLC_BLOCK) & (i * CMP_STRIDE + CMP_BLOCK > j * SLC_BLOCK))
    cover_t = jnp.asarray(cover.T, BF16)
    lane = np.arange(HEAD_DIM)[None, :]
    onehot = jnp.asarray((np.arange(seq)[:, None] // SLC_BLOCK == lane) | (lane == STAB_ROW), BF16)
    kv_spec = pl.BlockSpec((seq, HEAD_DIM), lambda b, g, q: (b, g))
    vt_spec = pl.BlockSpec((1, 1, seq // KC, HEAD_DIM, KC), lambda b, g, q: (b, g, 0, 0, 0))
    gw = NSA_GROUP * HEAD_DIM
    return pl.pallas_call(
        _nsa_attn_kernel,
        grid=(batch, NSA_KV_HEADS, n_q),
        in_specs=[
            pl.BlockSpec(memory_space=pltpu.SMEM),
            pl.BlockSpec((1, gw, tq), lambda b, g, q: (b, g, q)),
            pl.BlockSpec((n_cmp, HEAD_DIM), lambda b, g, q: (b * NSA_KV_HEADS + g, 0)),
            pl.BlockSpec((1, HEAD_DIM, n_cmp), lambda b, g, q: (b * NSA_KV_HEADS + g, 0, 0)),
            kv_spec, vt_spec, kv_spec, vt_spec,
            pl.BlockSpec((tq, V7X_LANES), lambda b, g, q: (b * n_q + q, 0)),
            pl.BlockSpec((tq, gw), lambda b, g, q: (b * n_q + q, g)),
            pl.BlockSpec((n_sel, n_cmp), lambda b, g, q: (0, 0)),
            pl.BlockSpec((seq, HEAD_DIM), lambda b, g, q: (0, 0)),
        ],
        out_specs=pl.BlockSpec((tq, gw), lambda b, g, q: (b * n_q + q, g)),
        out_shape=jax.ShapeDtypeStruct((batch * seq, NSA_WIDTH), BF16),
        scratch_shapes=[pltpu.VMEM((V7X_LANES, tq), F32),
                        pltpu.VMEM((2, NSA_GROUP, seq, TQ), F32),
                        pltpu.VMEM((2, NSA_GROUP, WINDOW + KC, TQ), F32)],
        compiler_params=_params(48, 3),
        name="nsa_attn",
    )(stab, qt, kcmp, vcmpt, ks, vst, kw, vwt, gates, sz, cover_t, onehot)


def _repack_kernel(w_ref, main_ref, gate_ref):
    w = w_ref[...]
    n_gate = 3 * NSA_HEADS
    main_ref[:, :NSA_WIDTH] = w[:, :NSA_WIDTH].astype(BF16)
    main_ref[:, NSA_WIDTH:] = w[:, NSA_WIDTH + n_gate:].astype(BF16)
    per_group = n_gate // NSA_KV_HEADS
    pad = jnp.zeros((w.shape[0], GATE_LANES - per_group), F32)
    pieces = []
    for g in range(NSA_KV_HEADS):
        pieces += [w[:, NSA_WIDTH + g * per_group:NSA_WIDTH + (g + 1) * per_group], pad]
    gate_ref[...] = jnp.concatenate(pieces, axis=1).astype(BF16)


def _repack_b_w_in(w):
    d, n = w.shape
    n_main = n - 3 * NSA_HEADS
    tr = 256
    return pl.pallas_call(
        _repack_kernel,
        grid=(d // tr,),
        in_specs=[pl.BlockSpec((tr, n), lambda i: (i, 0))],
        out_specs=[pl.BlockSpec((tr, n_main), lambda i: (i, 0)),
                   pl.BlockSpec((tr, V7X_LANES), lambda i: (i, 0))],
        out_shape=[jax.ShapeDtypeStruct((d, n_main), BF16),
                   jax.ShapeDtypeStruct((d, V7X_LANES), BF16)],
        compiler_params=_params(32, 1),
        name="repack_b_w_in",
    )(w)


def _gate_layout(v):
    per_group = NSA_GROUP * 3
    v = v.reshape(v.shape[:-1] + (NSA_KV_HEADS, per_group))
    v = jnp.pad(v, [(0, 0)] * (v.ndim - 1) + [(0, GATE_LANES - per_group)])
    return v.reshape(v.shape[:-2] + (NSA_KV_HEADS * GATE_LANES,))


def kernel(x, mem, a_norm, a_w_in, a_conv_w, a_conv_b, a_w_out, kv_norm, kv_w,
           cmp_pos_k, cmp_w1_k, cmp_w2_k, cmp_pos_v, cmp_w1_v, cmp_w2_v,
           kn_cmp, kn_slc, kn_win, b_norm, b_w_in, b_gate_bias, b_q_norm, b_w_out,
           mem_norm, mem_w_kv, mem_q_norm, mem_k_norm):
    batch, seq, d = x.shape
    conv_width = d - MEM_WIDTH
    assert a_norm.shape[0] == 1 and b_norm.shape[0] == 1 and seq % (TQ * TILES_PER_STEP) == 0
    x2d = x.reshape(batch * seq, d)
    mem2d = mem.reshape(batch * MEM_TOKENS, d)

    mk, mv = _mem_kv(mem2d, mem_norm, mem_w_kv.astype(BF16), mem_k_norm)

    y_conv, y_mem = _a_in(x2d, a_norm[0], a_w_in[0].astype(BF16), a_conv_w[0], a_conv_b[0],
                          mk[0], mv[0], mem_q_norm[0], seq)
    x1 = _out_proj(x2d, y_conv, y_mem, a_w_out[0].astype(BF16))

    kc, vc, ks, vst, kw, vwt = _nsa_kv(x1, kv_norm, kv_w.astype(BF16), kn_slc, kn_win, batch, seq)
    k_cmp = _compress(kc, cmp_pos_k, cmp_w1_k, cmp_w2_k, kn_cmp, True, batch, seq)
    v_cmp_t = _compress(vc, cmp_pos_v, cmp_w1_v, cmp_w2_v, kn_cmp, False, batch, seq)

    w_main, w_gate = _repack_b_w_in(b_w_in[0])
    q_gain_b = jnp.broadcast_to((b_q_norm[0] * (SCALE * LOG2E))[:, None], (HEAD_DIM, HEAD_DIM))
    qt, gates, sz, y_mem2 = _b_in(x1, b_norm[0], w_main, w_gate,
                                  _gate_layout(b_gate_bias[0])[None, :], q_gain_b,
                                  mk[1], mv[1], mem_q_norm[1], batch, seq)
    stab = (1.02 * HEAD_DIM * SCALE * LOG2E * jnp.max(jnp.abs(b_q_norm[0]))
            * jnp.maximum(jnp.max(jnp.abs(kn_slc)), jnp.max(jnp.abs(kn_win)))).reshape(1)
    y_nsa = _nsa_attn(stab, qt, k_cmp, v_cmp_t, ks, vst, kw, vwt, gates, sz, batch, seq)
    out = _out_proj(x1, y_nsa, y_mem2, b_w_out[0].astype(BF16))
    return out.reshape(batch, seq, d)
```

```python
import functools

import numpy as np
import jax
import jax.numpy as jnp
from jax import lax
from jax.experimental import pallas as pl
from jax.experimental.pallas import tpu as pltpu

HEAD_DIM = 128
MEM_TOKENS = 256
MEM_HEADS = 4
MEM_WIDTH = MEM_HEADS * HEAD_DIM
NSA_KV_HEADS = 4
NSA_GROUP = 3
NSA_HEADS = NSA_KV_HEADS * NSA_GROUP
NSA_WIDTH = NSA_HEADS * HEAD_DIM
KV_BRANCH_WIDTH = NSA_KV_HEADS * HEAD_DIM
CONV_SIZE = 3
CMP_BLOCK = 32
CMP_STRIDE = 16
CMP_HIDDEN = 256
SLC_BLOCK = 64
SLC_SHIFT = 6
SLC_TOPK = 16
WINDOW = 512
EPS = 1e-6
NEG = -1e30
FORCE = 1e4
SCALE = HEAD_DIM ** -0.5
LOG2E = 1.4426950408889634

V7X_VMEM_BYTES = 64 * 1024 * 1024
V7X_LANES = 128
BF16_SUBLANES = 16

TM = 1024
TM_RESIDENT = 512
HALO = BF16_SUBLANES
CONV_CW = 256
TQ = 256
TILES_PER_STEP = 8
KC = 256
GATE_LANES = 32
CMP_PER_TILE = TQ // CMP_STRIDE
STAB_ROW = 32
STAB_MAX = 56.0

BF16 = jnp.bfloat16
F32 = jnp.float32


def _params(vmem_mb, n_grid):
    return pltpu.CompilerParams(
        dimension_semantics=("arbitrary",) * n_grid,
        vmem_limit_bytes=vmem_mb * 1024 * 1024)


def _rms_rows(v, gain):
    return v * lax.rsqrt(jnp.mean(v * v, axis=-1, keepdims=True) + EPS) * gain


def _silu(v):
    half = 0.5 * v
    return half + half * jnp.tanh(half)


def _dot(a, b):
    return jnp.dot(a, b, preferred_element_type=F32)


def _dot_nt(a, b):
    return lax.dot_general(a, b, (((1,), (1,)), ((), ())), preferred_element_type=F32)


def _mem_attention(mq, mk_ref, mv_ref, q_gain):
    outs = []
    for h in range(MEM_HEADS):
        sl = slice(h * HEAD_DIM, (h + 1) * HEAD_DIM)
        q = _rms_rows(mq[:, sl], q_gain * (SCALE * LOG2E)).astype(BF16)
        s = _dot_nt(q, mk_ref[:, sl])
        e = jnp.exp2(s - jnp.max(s, axis=-1, keepdims=True))
        l = jnp.sum(e, axis=-1, keepdims=True)
        outs.append(_dot(e.astype(BF16), mv_ref[:, sl]) / l)
    return outs


def _mem_kv_kernel(mem_ref, g_ref, w_ref, kg_ref, mk_ref, mv_ref):
    h = _rms_rows(mem_ref[...], g_ref[0]).astype(BF16)
    kv = _dot(h, w_ref[0])
    for hd in range(MEM_HEADS):
        sl = slice(hd * HEAD_DIM, (hd + 1) * HEAD_DIM)
        mk_ref[0, :, sl] = _rms_rows(kv[:, sl], kg_ref[0]).astype(BF16)
    mv_ref[0] = kv[:, MEM_WIDTH:].astype(BF16)


def _mem_kv(mem2d, mem_norm, w_kv, k_gain):
    n_layers, d = mem_norm.shape
    rows = mem2d.shape[0]
    tm = min(TM, rows)
    out = jax.ShapeDtypeStruct((n_layers, rows, MEM_WIDTH), BF16)
    return pl.pallas_call(
        _mem_kv_kernel,
        grid=(n_layers, rows // tm),
        in_specs=[
            pl.BlockSpec((tm, d), lambda l, i: (i, 0)),
            pl.BlockSpec((1, 1, d), lambda l, i: (l, 0, 0)),
            pl.BlockSpec((1, d, 2 * MEM_WIDTH), lambda l, i: (l, 0, 0)),
            pl.BlockSpec((1, 1, HEAD_DIM), lambda l, i: (l, 0, 0)),
        ],
        out_specs=[pl.BlockSpec((1, tm, MEM_WIDTH), lambda l, i: (l, i, 0))] * 2,
        out_shape=[out, out],
        compiler_params=_params(40, 2),
        name="mem_kv",
    )(mem2d, mem_norm[:, None, :], w_kv, k_gain[:, None, :])


def _a_in_kernel(x_ref, xh_ref, g_ref, w_ref, cw_ref, cb_ref, mk_ref, mv_ref, qg_ref,
                 yc_ref, ym_ref, *, tiles_per_seq):
    conv_width = yc_ref.shape[1]
    keep = (pl.program_id(0) % tiles_per_seq != 0).astype(F32)
    h_halo = (_rms_rows(xh_ref[...], g_ref[...]) * keep).astype(BF16)
    h = _rms_rows(x_ref[...], g_ref[...]).astype(BF16)
    h_full = jnp.concatenate([h_halo, h], axis=0)
    for c in range(conv_width // CONV_CW):
        ch = slice(c * CONV_CW, (c + 1) * CONV_CW)

        def part(k, lhs):
            return _dot(lhs, w_ref[:, k * conv_width + c * CONV_CW:
                                   k * conv_width + (c + 1) * CONV_CW])

        u = part(1, h_full) * part(2, h_full)
        taps = cw_ref[:, ch]
        y = (taps[0:1] * pltpu.roll(u, 2, axis=0)[HALO:]
             + taps[1:2] * pltpu.roll(u, 1, axis=0)[HALO:]
             + taps[2:3] * u[HALO:])
        yc_ref[:, ch] = (part(0, h) * (y + cb_ref[:, ch]) * _silu(part(3, h))).astype(BF16)
    acc = _dot(h, w_ref[:, 4 * conv_width:])
    outs = _mem_attention(acc[:, :MEM_WIDTH], mk_ref, mv_ref, qg_ref[...])
    for hd in range(MEM_HEADS):
        sl = slice(hd * HEAD_DIM, (hd + 1) * HEAD_DIM)
        ym_ref[:, sl] = (outs[hd] * _silu(acc[:, MEM_WIDTH + hd * HEAD_DIM:
                                              MEM_WIDTH + (hd + 1) * HEAD_DIM])).astype(BF16)


def _a_in(x2d, norm, w, conv_w, conv_b, mk, mv, q_gain, seq):
    t, d = x2d.shape
    conv_width = conv_w.shape[1]
    assert w.shape[1] == 4 * conv_width + 2 * MEM_WIDTH and conv_width % CONV_CW == 0
    tm = TM_RESIDENT
    tiles_per_seq = seq // tm
    halo_blocks = tm // HALO
    kern = functools.partial(_a_in_kernel, tiles_per_seq=tiles_per_seq)
    const = lambda i: (0, 0)
    row = lambda i: (i, 0)
    mem_idx = lambda i: (i // tiles_per_seq, 0)
    return pl.pallas_call(
        kern,
        grid=(t // tm,),
        in_specs=[
            pl.BlockSpec((tm, d), row),
            pl.BlockSpec((HALO, d), lambda i: (jnp.maximum(i * halo_blocks - 1, 0), 0)),
            pl.BlockSpec((1, d), const),
            pl.BlockSpec(w.shape, const, pipeline_mode=pl.Buffered(1)),
            pl.BlockSpec((CONV_SIZE, conv_width), const),
            pl.BlockSpec((1, conv_width), const),
            pl.BlockSpec((MEM_TOKENS, MEM_WIDTH), mem_idx),
            pl.BlockSpec((MEM_TOKENS, MEM_WIDTH), mem_idx),
            pl.BlockSpec((1, HEAD_DIM), const),
        ],
        out_specs=[pl.BlockSpec((tm, conv_width), row), pl.BlockSpec((tm, MEM_WIDTH), row)],
        out_shape=[jax.ShapeDtypeStruct((t, conv_width), BF16),
                   jax.ShapeDtypeStruct((t, MEM_WIDTH), BF16)],
        compiler_params=_params(56, 1),
        name="a_in",
    )(x2d, x2d, norm[None, :], w, conv_w, conv_b[None, :], mk, mv, q_gain[None, :])


def _out_proj_kernel(res_ref, y1_ref, y2_ref, w1_ref, w2_ref, o_ref):
    o_ref[...] = res_ref[...] + _dot(y1_ref[...], w1_ref[...]) + _dot(y2_ref[...], w2_ref[...])


def _out_proj(res, y1, y2, w):
    t, d = res.shape
    tm = TM
    n1, n2 = y1.shape[1], y2.shape[1]
    assert n1 % n2 == 0 and n1 + n2 == w.shape[0]
    once = pl.Buffered(1)
    return pl.pallas_call(
        _out_proj_kernel,
        grid=(t // tm,),
        in_specs=[
            pl.BlockSpec((tm, d), lambda i: (i, 0)),
            pl.BlockSpec((tm, n1), lambda i: (i, 0)),
            pl.BlockSpec((tm, n2), lambda i: (i, 0)),
            pl.BlockSpec((n1, d), lambda i: (0, 0), pipeline_mode=once),
            pl.BlockSpec((n2, d), lambda i: (n1 // n2, 0), pipeline_mode=once),
        ],
        out_specs=pl.BlockSpec((tm, d), lambda i: (i, 0)),
        out_shape=jax.ShapeDtypeStruct((t, d), F32),
        compiler_params=_params(57, 1),
        name="out_proj",
    )(res, y1, y2, w, w)


def _store_transposed(dst_ref, val):
    vt = val.T.astype(BF16)
    for g in range(NSA_KV_HEADS):
        for cc in range(val.shape[0] // KC):
            dst_ref[0, g, cc] = vt[g * HEAD_DIM:(g + 1) * HEAD_DIM, cc * KC:(cc + 1) * KC]


def _nsa_kv_kernel(x_ref, g_ref, w_ref, kns_ref, knw_ref,
                   kc_ref, vc_ref, ks_ref, vst_ref, kw_ref, vwt_ref):
    bw = KV_BRANCH_WIDTH
    h = _rms_rows(x_ref[...], g_ref[...]).astype(BF16)

    def branch(j):
        return _dot(h, w_ref[:, j * bw:(j + 1) * bw])

    def normed_keys(acc, gain):
        return jnp.concatenate(
            [_rms_rows(acc[:, g * HEAD_DIM:(g + 1) * HEAD_DIM], gain).astype(BF16)
             for g in range(NSA_KV_HEADS)], axis=1)

    kc_ref[...] = branch(0)
    vc_ref[...] = branch(1)
    ks_ref[...] = normed_keys(branch(2), kns_ref[...])
    _store_transposed(vst_ref, branch(3))
    kw_ref[...] = normed_keys(branch(4), knw_ref[...])
    _store_transposed(vwt_ref, branch(5))


def _nsa_kv(x2d, norm, w, kn_slc, kn_win, batch, seq):
    t, d = x2d.shape
    bw = KV_BRANCH_WIDTH
    tm = TM_RESIDENT
    tiles_per_seq = seq // tm
    row = lambda i: (i, 0)
    const = lambda i: (0, 0)
    vt_spec = pl.BlockSpec((1, NSA_KV_HEADS, tm // KC, HEAD_DIM, KC),
                           lambda i: (i // tiles_per_seq, 0, i % tiles_per_seq, 0, 0))
    vt_shape = jax.ShapeDtypeStruct((batch, NSA_KV_HEADS, seq // KC, HEAD_DIM, KC), BF16)
    return pl.pallas_call(
        _nsa_kv_kernel,
        grid=(t // tm,),
        in_specs=[
            pl.BlockSpec((tm, d), row),
            pl.BlockSpec((1, d), const),
            pl.BlockSpec((d, 6 * bw), const, pipeline_mode=pl.Buffered(1)),
            pl.BlockSpec((1, HEAD_DIM), const),
            pl.BlockSpec((1, HEAD_DIM), const),
        ],
        out_specs=[pl.BlockSpec((tm, bw), row), pl.BlockSpec((tm, bw), row),
                   pl.BlockSpec((tm, bw), row), vt_spec,
                   pl.BlockSpec((tm, bw), row), vt_spec],
        out_shape=[jax.ShapeDtypeStruct((t, bw), F32), jax.ShapeDtypeStruct((t, bw), F32),
                   jax.ShapeDtypeStruct((t, bw), BF16), vt_shape,
                   jax.ShapeDtypeStruct((t, bw), BF16), vt_shape],
        compiler_params=_params(44, 1),
        name="nsa_kv",
    )(x2d, norm[None, :], w, kn_slc[None, :], kn_win[None, :])


def _compress_kernel(*refs, is_key):
    x_refs = refs[:NSA_KV_HEADS]
    pos_ref, w1_ref, w2_ref, kn_ref, o_ref = refs[NSA_KV_HEADS:]
    n_chunks = x_refs[0].shape[0] // CMP_STRIDE
    pairs = CMP_STRIDE // 2
    first = second = None
    for p in range(pairs):
        lhs_a, lhs_b = [], []
        for l in (2 * p, 2 * p + 1):
            xg = jnp.concatenate([x_ref[pl.ds(l, n_chunks, stride=CMP_STRIDE), :]
                                  for x_ref in x_refs], axis=0)
            lhs_a.append((xg + pos_ref[l:l + 1, :]).astype(BF16))
            lhs_b.append((xg + pos_ref[CMP_STRIDE + l:CMP_STRIDE + l + 1, :]).astype(BF16))
        da = _dot(jnp.concatenate(lhs_a, axis=1), w1_ref[p])
        db = _dot(jnp.concatenate(lhs_b, axis=1), w1_ref[pairs + p])
        first = da if first is None else first + da
        second = db if second is None else second + db
    rows = first.shape[0]
    hid = _silu(first + pltpu.roll(second, rows - 1, axis=0))
    o = _dot(hid.astype(BF16), w2_ref[...])
    if is_key:
        o = _rms_rows(o, kn_ref[...])
    chunk = lax.broadcasted_iota(jnp.int32, o.shape, 0) & (n_chunks - 1)
    o = jnp.where(chunk == n_chunks - 1, 0.0, o)
    if is_key:
        o_ref[...] = o.astype(BF16)
    else:
        for g in range(NSA_KV_HEADS):
            o_ref[g] = o[g * n_chunks:(g + 1) * n_chunks, :].T.astype(BF16)


def _compress(t2d, pos, w1, w2, kn, is_key, batch, seq):
    assert t2d.shape[1] == NSA_KV_HEADS * HEAD_DIM
    n_chunks = seq // CMP_STRIDE
    rows = NSA_KV_HEADS * n_chunks
    w1p = w1.reshape(CMP_BLOCK // 2, 2 * HEAD_DIM, CMP_HIDDEN).astype(BF16)
    kern = functools.partial(_compress_kernel, is_key=is_key)
    if is_key:
        out_spec = pl.BlockSpec((rows, HEAD_DIM), lambda b: (b, 0))
        out_shape = jax.ShapeDtypeStruct((batch * rows, HEAD_DIM), BF16)
    else:
        out_spec = pl.BlockSpec((NSA_KV_HEADS, HEAD_DIM, n_chunks), lambda b: (b, 0, 0))
        out_shape = jax.ShapeDtypeStruct((batch * NSA_KV_HEADS, HEAD_DIM, n_chunks), BF16)
    return pl.pallas_call(
        kern,
        grid=(batch,),
        in_specs=[pl.BlockSpec((seq, HEAD_DIM), functools.partial(lambda b, g: (b, g), g=g))
                  for g in range(NSA_KV_HEADS)] + [
            pl.BlockSpec((CMP_BLOCK, HEAD_DIM), lambda b: (0, 0)),
            pl.BlockSpec((CMP_BLOCK // 2, 2 * HEAD_DIM, CMP_HIDDEN), lambda b: (0, 0, 0)),
            pl.BlockSpec((CMP_HIDDEN, HEAD_DIM), lambda b: (0, 0)),
            pl.BlockSpec((1, HEAD_DIM), lambda b: (0, 0)),
        ],
        out_specs=out_spec,
        out_shape=out_shape,
        compiler_params=_params(32, 1),
        name="compress_k" if is_key else "compress_v",
    )(*([t2d] * NSA_KV_HEADS), pos, w1p, w2.astype(BF16), kn[None, :])


def _b_in_kernel(x_ref, g_ref, w_ref, wg_ref, gb_ref, qg_ref, mk_ref, mv_ref, mqg_ref,
                 qt_ref, gate_ref, sz_ref, ym_ref):
    tm = x_ref.shape[0]
    h = _rms_rows(x_ref[...], g_ref[...]).astype(BF16)
    gate_ref[...] = jax.nn.sigmoid(_dot(h, wg_ref[...]) + gb_ref[...])

    def cols(start, width):
        return _dot(h, w_ref[:, start:start + width])

    step = 4 * HEAD_DIM
    gain = jnp.tile(qg_ref[...], (1, tm // HEAD_DIM))
    for c in range(NSA_WIDTH // step):
        acc = cols(c * step, step)
        for hd in range(step // HEAD_DIM):
            qt = acc[:, hd * HEAD_DIM:(hd + 1) * HEAD_DIM].T
            inv = lax.rsqrt(jnp.mean(qt * qt, axis=0, keepdims=True) + EPS)
            row0 = c * step + hd * HEAD_DIM
            qt_ref[0, row0:row0 + HEAD_DIM, :] = (qt * inv * gain).astype(BF16)
    for c in range(NSA_WIDTH // step):
        sz_ref[:, c * step:(c + 1) * step] = _silu(cols(NSA_WIDTH + c * step, step)).astype(BF16)
    outs = _mem_attention(cols(2 * NSA_WIDTH, MEM_WIDTH), mk_ref, mv_ref, mqg_ref[...])
    mz = cols(2 * NSA_WIDTH + MEM_WIDTH, MEM_WIDTH)
    for hd in range(MEM_HEADS):
        sl = slice(hd * HEAD_DIM, (hd + 1) * HEAD_DIM)
        ym_ref[:, sl] = (outs[hd] * _silu(mz[:, sl])).astype(BF16)


def _b_in(x2d, norm, w_main, w_gate, gate_bias, q_gain_b, mk, mv, mem_q_gain, batch, seq):
    t, d = x2d.shape
    tm = TM_RESIDENT
    tiles_per_seq = seq // tm
    const = lambda i: (0, 0)
    row = lambda i: (i, 0)
    mem_idx = lambda i: (i // tiles_per_seq, 0)
    once = pl.Buffered(1)
    return pl.pallas_call(
        _b_in_kernel,
        grid=(t // tm,),
        in_specs=[
            pl.BlockSpec((tm, d), row),
            pl.BlockSpec((1, d), const),
            pl.BlockSpec(w_main.shape, const, pipeline_mode=once),
            pl.BlockSpec((d, V7X_LANES), const, pipeline_mode=once),
            pl.BlockSpec((1, V7X_LANES), const),
            pl.BlockSpec((HEAD_DIM, HEAD_DIM), const),
            pl.BlockSpec((MEM_TOKENS, MEM_WIDTH), mem_idx),
            pl.BlockSpec((MEM_TOKENS, MEM_WIDTH), mem_idx),
            pl.BlockSpec((1, HEAD_DIM), const),
        ],
        out_specs=[
            pl.BlockSpec((1, NSA_WIDTH, tm), lambda i: (i // tiles_per_seq, 0, i % tiles_per_seq)),
            pl.BlockSpec((tm, V7X_LANES), row),
            pl.BlockSpec((tm, NSA_WIDTH), row),
            pl.BlockSpec((tm, MEM_WIDTH), row),
        ],
        out_shape=[jax.ShapeDtypeStruct((batch, NSA_WIDTH, seq), BF16),
                   jax.ShapeDtypeStruct((t, V7X_LANES), F32),
                   jax.ShapeDtypeStruct((t, NSA_WIDTH), BF16),
                   jax.ShapeDtypeStruct((t, MEM_WIDTH), BF16)],
        compiler_params=_params(48, 1),
        name="b_in",
    )(x2d, norm[None, :], w_main, w_gate, gate_bias, q_gain_b, mk, mv, mem_q_gain[None, :])


def _attn_block(qi, fast, row0, q3, gates, stab, kcmp_ref, vcmpt_ref, ks_ref, vst_ref, kw_ref, vwt_ref,
                sz_ref, cover_ref, onehot_ref, y_ref, ssel_ref, swin_ref):
    n = NSA_GROUP * TQ
    qs = qi * TQ
    par = qi % 2
    q = [q3[j * HEAD_DIM:(j + 1) * HEAD_DIM] for j in range(NSA_GROUP)]
    k_loc = lax.broadcasted_iota(jnp.int32, (KC, TQ), 0)
    t_loc = lax.broadcasted_iota(jnp.int32, (KC, TQ), 1)
    causal = k_loc <= t_loc
    band = k_loc > t_loc
    sel_chunks = range(qi + 1)
    win_chunks = range(max(qi - WINDOW // KC, 0), qi + 1)
    n_allowed = (qs + TQ - 1) // SLC_BLOCK + 1
    use_bias = n_allowed > SLC_TOPK

    ones_rows = jnp.ones((BF16_SUBLANES, KC), BF16)

    def scores_pass(is_sel, j, sel_bias=None):
        k_ref, s_ref = (ks_ref, ssel_ref) if is_sel else (kw_ref, swin_ref)
        qj = q[j]
        augmented = fast or sel_bias is not None
        if augmented:
            extra = jnp.zeros((HEAD_DIM, TQ), F32) if sel_bias is None else jnp.concatenate(
                [sel_bias, jnp.zeros((HEAD_DIM - sel_bias.shape[0], TQ), F32)], axis=0)
            if fast:
                row = lax.broadcasted_iota(jnp.int32, (HEAD_DIM, TQ), 0)
                extra = jnp.where(row == STAB_ROW, -stab, extra)
            qj = jnp.concatenate([qj, extra.astype(BF16)], axis=0)
        m = None
        for i, ck in enumerate(sel_chunks if is_sel else win_chunks):
            k = k_ref[ck * KC:(ck + 1) * KC, :]
            if augmented:
                k = jnp.concatenate([k, onehot_ref[ck * KC:(ck + 1) * KC, :]], axis=1)
            sc = _dot(k, qj)
            if not is_sel and ck == qi - WINDOW // KC:
                sc = jnp.where(band, sc, NEG)
            if ck == qi:
                sc = jnp.where(causal, sc, NEG)
            s_ref[par, j, i * KC:(i + 1) * KC, :] = sc
            if not fast:
                m_c = jnp.max(sc, axis=0, keepdims=True)
                m = m_c if m is None else jnp.maximum(m, m_c)
        return m

    def values_pass(is_sel, j, m):
        vt_ref, s_ref = (vst_ref, ssel_ref) if is_sel else (vwt_ref, swin_ref)
        l = acc = None
        for i, ck in enumerate(sel_chunks if is_sel else win_chunks):
            sc = s_ref[par, j, i * KC:(i + 1) * KC, :]
            vt = vt_ref[0, 0, ck]
            if fast:
                pe = jnp.exp2(sc)
                vt = jnp.concatenate([vt, ones_rows], axis=0)
            else:
                pe = jnp.exp2(sc - m)
                l_c = jnp.sum(pe, axis=0, keepdims=True)
                l = l_c if l is None else l + l_c
            pv = _dot(vt, pe.astype(BF16))
            acc = pv if acc is None else acc + pv
        if fast:
            acc, l = acc[:HEAD_DIM], acc[HEAD_DIM:HEAD_DIM + 1]
        return acc, l

    def merge(j, o_cmp_j, sel_j, win_j):
        yt = (gates[3 * j:3 * j + 1] * o_cmp_j
              + (gates[3 * j + 1:3 * j + 2] / sel_j[1]) * sel_j[0]
              + (gates[3 * j + 2:3 * j + 3] / win_j[1]) * win_j[0])
        ch = slice(j * HEAD_DIM, (j + 1) * HEAD_DIM)
        rows = slice(row0, row0 + TQ)
        y_ref[rows, ch] = (yt.T * sz_ref[rows, ch].astype(F32)).astype(BF16)

    n_cmp = kcmp_ref.shape[0]
    n_vis = min(n_cmp, CMP_PER_TILE * (qi + 1))
    s_cmp = _dot(kcmp_ref[0:n_vis, :], jnp.concatenate(q, axis=1))
    m_win = [scores_pass(False, 0)]
    top = max(n_vis - 2 * CMP_PER_TILE, 0)
    c_idx = top + lax.broadcasted_iota(jnp.int32, (n_vis - top, n), 0)
    t_idx = qs + (lax.broadcasted_iota(jnp.int32, (n_vis - top, n), 1) & (TQ - 1))
    slab_ok = c_idx * CMP_STRIDE + (CMP_BLOCK - 1) <= t_idx

    def with_slab(v, fill):
        slab = jnp.where(slab_ok, v[top:], fill)
        return slab if top == 0 else jnp.concatenate([v[:top], slab], axis=0)

    sm = with_slab(s_cmp, NEG)
    e = jnp.exp2(sm - jnp.max(sm, axis=0, keepdims=True))
    p = with_slab(e / jnp.sum(e, axis=0, keepdims=True), 0.0)
    if n_vis < n_cmp:
        p = jnp.concatenate([p, jnp.zeros((n_cmp - n_vis, n), F32)], axis=0)
    o_cmp = _dot(vcmpt_ref[0], p.astype(BF16))
    m_win.append(scores_pass(False, 1))

    sel_bias = None
    if use_bias:
        imp = p[:, 0:TQ] + p[:, TQ:2 * TQ] + p[:, 2 * TQ:3 * TQ]
        hi = imp.astype(BF16)
        r1 = imp - hi.astype(F32)
        mid = r1.astype(BF16)
        lo = (r1 - mid.astype(F32)).astype(BF16)
        cover = cover_ref[...]
        score = _dot(cover, hi) + _dot(cover, mid) + _dot(cover, lo)
        n_sel = score.shape[0]
        j_idx = lax.broadcasted_iota(jnp.int32, (n_sel, TQ), 0)
        tq_idx = qs + lax.broadcasted_iota(jnp.int32, (n_sel, TQ), 1)
        cur = tq_idx >> SLC_SHIFT
        allowed = j_idx * SLC_BLOCK <= tq_idx
        forced = (j_idx == 0) | (j_idx == cur) | (j_idx == cur - 1)
        score = jnp.where(forced, FORCE, jnp.where(allowed, score, NEG))
        rank = jnp.zeros((n_sel, TQ), jnp.int32)
        for jp in range(n_allowed):
            row = score[jp:jp + 1, :]
            before = (row > score) | ((row == score) & (j_idx > jp))
            rank = rank + before.astype(jnp.int32)
        sel_bias = jnp.where(allowed & (rank < SLC_TOPK), 0.0, NEG)

    win = [values_pass(False, 0, m_win[0])]
    m_win.append(scores_pass(False, 2))
    win.append(values_pass(False, 1, m_win[1]))
    m_sel = [scores_pass(True, 0, sel_bias)]
    win.append(values_pass(False, 2, m_win[2]))
    for j in range(NSA_GROUP):
        if j + 1 < NSA_GROUP:
            m_sel.append(scores_pass(True, j + 1, sel_bias))
        merge(j, o_cmp[:, j * TQ:(j + 1) * TQ], values_pass(True, j, m_sel[j]), win[j])


def _nsa_attn_kernel(stab_ref, qt_ref, kcmp_ref, vcmpt_ref, ks_ref, vst_ref, kw_ref, vwt_ref,
                     gate_ref, sz_ref, cover_ref, onehot_ref, y_ref, gt_ref, ssel_ref, swin_ref):
    g = pl.program_id(1)
    step = pl.program_id(2)

    def tiles_body(first, fast):
        gt_ref[...] = gate_ref[...].T
        gates = gt_ref[pl.ds(pl.multiple_of(g * GATE_LANES, GATE_LANES), GATE_LANES), :]
        for r in range(TILES_PER_STEP):
            cols = slice(r * TQ, (r + 1) * TQ)
            _attn_block(first + r, fast, r * TQ, qt_ref[0, :, cols], gates[:, cols], stab,
                        kcmp_ref, vcmpt_ref, ks_ref, vst_ref, kw_ref, vwt_ref, sz_ref, cover_ref,
                        onehot_ref, y_ref, ssel_ref, swin_ref)

    stab = stab_ref[0]
    for s in range(ks_ref.shape[0] // (TQ * TILES_PER_STEP)):
        for fast in (True, False):
            pl.when((step == s) & ((stab <= STAB_MAX) == fast))(
                functools.partial(tiles_body, s * TILES_PER_STEP, fast))


def _nsa_attn(stab, qt, kcmp, vcmpt, ks, vst, kw, vwt, gates, sz, batch, seq):
    tq = TQ * TILES_PER_STEP
    n_q = seq // tq
    n_cmp = seq // CMP_STRIDE
    n_sel = seq // SLC_BLOCK
    n = NSA_GROUP * TQ
    i = np.arange(n_cmp)[:, None]
    j = np.arange(n_sel)[None, :]
    cover = ((i * CMP_STRIDE < (j + 1) * SLC_BLOCK) & (i * CMP_STRIDE + CMP_BLOCK > j * SLC_BLOCK))
    cover_t = jnp.asarray(cover.T, BF16)
    lane = np.arange(HEAD_DIM)[None, :]
    onehot = jnp.asarray((np.arange(seq)[:, None] // SLC_BLOCK == lane) | (lane == STAB_ROW), BF16)
    kv_spec = pl.BlockSpec((seq, HEAD_DIM), lambda b, g, q: (b, g))
    vt_spec = pl.BlockSpec((1, 1, seq // KC, HEAD_DIM, KC), lambda b, g, q: (b, g, 0, 0, 0))
    gw = NSA_GROUP * HEAD_DIM
    return pl.pallas_call(
        _nsa_attn_kernel,
        grid=(batch, NSA_KV_HEADS, n_q),
        in_specs=[
            pl.BlockSpec(memory_space=pltpu.SMEM),
            pl.BlockSpec((1, gw, tq), lambda b, g, q: (b, g, q)),
            pl.BlockSpec((n_cmp, HEAD_DIM), lambda b, g, q: (b * NSA_KV_HEADS + g, 0)),
            pl.BlockSpec((1, HEAD_DIM, n_cmp), lambda b, g, q: (b * NSA_KV_HEADS + g, 0, 0)),
            kv_spec, vt_spec, kv_spec, vt_spec,
            pl.BlockSpec((tq, V7X_LANES), lambda b, g, q: (b * n_q + q, 0)),
            pl.BlockSpec((tq, gw), lambda b, g, q: (b * n_q + q, g)),
            pl.BlockSpec((n_sel, n_cmp), lambda b, g, q: (0, 0)),
            pl.BlockSpec((seq, HEAD_DIM), lambda b, g, q: (0, 0)),
        ],
        out_specs=pl.BlockSpec((tq, gw), lambda b, g, q: (b * n_q + q, g)),
        out_shape=jax.ShapeDtypeStruct((batch * seq, NSA_WIDTH), BF16),
        scratch_shapes=[pltpu.VMEM((V7X_LANES, tq), F32),
                        pltpu.VMEM((2, NSA_GROUP, seq, TQ), F32),
                        pltpu.VMEM((2, NSA_GROUP, WINDOW + KC, TQ), F32)],
        compiler_params=_params(48, 3),
        name="nsa_attn",
    )(stab, qt, kcmp, vcmpt, ks, vst, kw, vwt, gates, sz, cover_t, onehot)


def _repack_kernel(w_ref, main_ref, gate_ref):
    w = w_ref[...]
    n_gate = 3 * NSA_HEADS
    main_ref[:, :NSA_WIDTH] = w[:, :NSA_WIDTH].astype(BF16)
    main_ref[:, NSA_WIDTH:] = w[:, NSA_WIDTH + n_gate:].astype(BF16)
    per_group = n_gate // NSA_KV_HEADS
    pad = jnp.zeros((w.shape[0], GATE_LANES - per_group), F32)
    pieces = []
    for g in range(NSA_KV_HEADS):
        pieces += [w[:, NSA_WIDTH + g * per_group:NSA_WIDTH + (g + 1) * per_group], pad]
    gate_ref[...] = jnp.concatenate(pieces, axis=1).astype(BF16)


def _repack_b_w_in(w):
    d, n = w.shape
    n_main = n - 3 * NSA_HEADS
    tr = 256
    return pl.pallas_call(
        _repack_kernel,
        grid=(d // tr,),
        in_specs=[pl.BlockSpec((tr, n), lambda i: (i, 0))],
        out_specs=[pl.BlockSpec((tr, n_main), lambda i: (i, 0)),
                   pl.BlockSpec((tr, V7X_LANES), lambda i: (i, 0))],
        out_shape=[jax.ShapeDtypeStruct((d, n_main), BF16),
                   jax.ShapeDtypeStruct((d, V7X_LANES), BF16)],
        compiler_params=_params(32, 1),
        name="repack_b_w_in",
    )(w)


def _gate_layout(v):
    per_group = NSA_GROUP * 3
    v = v.reshape(v.shape[:-1] + (NSA_KV_HEADS, per_group))
    v = jnp.pad(v, [(0, 0)] * (v.ndim - 1) + [(0, GATE_LANES - per_group)])
    return v.reshape(v.shape[:-2] + (NSA_KV_HEADS * GATE_LANES,))


def kernel(x, mem, a_norm, a_w_in, a_conv_w, a_conv_b, a_w_out, kv_norm, kv_w,
           cmp_pos_k, cmp_w1_k, cmp_w2_k, cmp_pos_v, cmp_w1_v, cmp_w2_v,
           kn_cmp, kn_slc, kn_win, b_norm, b_w_in, b_gate_bias, b_q_norm, b_w_out,
           mem_norm, mem_w_kv, mem_q_norm, mem_k_norm):
    batch, seq, d = x.shape
    conv_width = d - MEM_WIDTH
    assert a_norm.shape[0] == 1 and b_norm.shape[0] == 1 and seq % (TQ * TILES_PER_STEP) == 0
    x2d = x.reshape(batch * seq, d)
    mem2d = mem.reshape(batch * MEM_TOKENS, d)

    mk, mv = _mem_kv(mem2d, mem_norm, mem_w_kv.astype(BF16), mem_k_norm)

    y_conv, y_mem = _a_in(x2d, a_norm[0], a_w_in[0].astype(BF16), a_conv_w[0], a_conv_b[0],
                          mk[0], mv[0], mem_q_norm[0], seq)
    x1 = _out_proj(x2d, y_conv, y_mem, a_w_out[0].astype(BF16))

    kc, vc, ks, vst, kw, vwt = _nsa_kv(x1, kv_norm, kv_w.astype(BF16), kn_slc, kn_win, batch, seq)
    k_cmp = _compress(kc, cmp_pos_k, cmp_w1_k, cmp_w2_k, kn_cmp, True, batch, seq)
    v_cmp_t = _compress(vc, cmp_pos_v, cmp_w1_v, cmp_w2_v, kn_cmp, False, batch, seq)

    w_main, w_gate = _repack_b_w_in(b_w_in[0])
    q_gain_b = jnp.broadcast_to((b_q_norm[0] * (SCALE * LOG2E))[:, None], (HEAD_DIM, HEAD_DIM))
    qt, gates, sz, y_mem2 = _b_in(x1, b_norm[0], w_main, w_gate,
                                  _gate_layout(b_gate_bias[0])[None, :], q_gain_b,
                                  mk[1], mv[1], mem_q_norm[1], batch, seq)
    stab = (1.02 * HEAD_DIM * SCALE * LOG2E * jnp.max(jnp.abs(b_q_norm[0]))
            * jnp.maximum(jnp.max(jnp.abs(kn_slc)), jnp.max(jnp.abs(kn_win)))).reshape(1)
    y_nsa = _nsa_attn(stab, qt, k_cmp, v_cmp_t, ks, vst, kw, vwt, gates, sz, batch, seq)
    out = _out_proj(x1, y_nsa, y_mem2, b_w_out[0].astype(BF16))
    return out.reshape(batch, seq, d)
```
